```python
import math
import jax, jax.numpy as jnp
from jax import lax
import numpy as np

D_MODEL = 1024
BATCH = 4
SEQ = 4096
DEPTH = 1
DEC_BATCH = 16
DEC_SEQ = 16
PAST_LEN = 2048

CHUNK = 64
GLA_HEADS = 4
GLA_DK = 128
GLA_DV = 256
GLA_RANK = 16
GLA_TAU = 16.0
SWA_HEADS = 16
SWA_KV_HEADS = 4
SWA_HD = 64
SWA_GROUP = SWA_HEADS // SWA_KV_HEADS
WINDOW = 128
WIN_CHUNKS = WINDOW // CHUNK
SWA_KEYS = (WIN_CHUNKS + 1) * CHUNK
N_BUCKETS = 32
MAX_DISTANCE = 128
D_FF = 2816
EPS = 1e-6

GLA_QK = GLA_HEADS * GLA_DK
GLA_V = GLA_HEADS * GLA_DV
SWA_Q = SWA_HEADS * SWA_HD
SWA_KV = SWA_KV_HEADS * SWA_HD
IN_SPLITS = (GLA_QK, GLA_QK, GLA_V, GLA_V, GLA_RANK, SWA_Q, SWA_KV, SWA_KV, D_MODEL, D_MODEL)
IN_WIDTH = 2 * GLA_QK + 2 * GLA_V + GLA_RANK + SWA_Q + 2 * SWA_KV + 2 * D_MODEL

kernel_name = "hybrid_gla_swa_macaron_stream_step"


def rms_norm(x, g):
    xf = x.astype(jnp.float32)
    y = xf * lax.rsqrt(jnp.mean(xf * xf, axis=-1, keepdims=True) + EPS)
    return (y * g.astype(jnp.float32)).astype(x.dtype)


def swiglu_ffn(x, w_up, w_down):
    a, b = jnp.split(x @ w_up, 2, axis=-1)
    return (jax.nn.silu(a) * b) @ w_down


def t5_bucket(rel):
    nb = N_BUCKETS // 2
    ret = jnp.where(rel > 0, nb, 0)
    n = jnp.abs(rel)
    max_exact = nb // 2
    nf = jnp.maximum(n, 1).astype(jnp.float32)
    large = max_exact + (jnp.log(nf / max_exact) / math.log(MAX_DISTANCE / max_exact)
                         * (nb - max_exact)).astype(jnp.int32)
    large = jnp.minimum(large, nb - 1)
    return ret + jnp.where(n < max_exact, n, large)


def rel_pos_bias(rel_bias, qpos, kpos):
    bucket = t5_bucket(kpos[None, :] - qpos[:, None])
    b = jnp.transpose(rel_bias[bucket], (2, 0, 1)).astype(jnp.float32)
    return b.reshape(SWA_KV_HEADS, SWA_GROUP, qpos.shape[0], kpos.shape[0])


def window_mask(qpos, kpos):
    qc = (qpos // CHUNK)[..., :, None]
    kc = (kpos // CHUNK)[..., None, :]
    return (kpos[..., None, :] >= 0) & (kc <= qc) & (kc >= qc - WIN_CHUNKS)


def sink_attention(q, k, v, bias, sinks, mask):
    qg = q.reshape(q.shape[:-2] + (SWA_KV_HEADS, SWA_GROUP, SWA_HD))
    s = jnp.einsum('...qkgd,...skd->...kgqs', qg, k).astype(jnp.float32) * (SWA_HD ** -0.5) + bias
    s = jnp.where(mask, s, -jnp.inf)
    sink = sinks.astype(jnp.float32).reshape(SWA_KV_HEADS, SWA_GROUP, 1, 1)
    m = jnp.maximum(jnp.max(s, axis=-1, keepdims=True), sink)
    p = jnp.exp(s - m)
    probs = p / (jnp.sum(p, axis=-1, keepdims=True) + jnp.exp(sink - m))
    out = jnp.einsum('...kgqs,...skd->...qkgd', probs.astype(v.dtype), v)
    return out.reshape(out.shape[:-3] + (SWA_Q,))


def gla_chunk(S, q, k, v, g):
    L = q.shape[1]
    b = jnp.cumsum(g, axis=1)
    inter = jnp.einsum('blhk,bhkv->blhv', q * jnp.exp(b), S)
    causal = jnp.tril(jnp.ones((L, L), dtype=bool))[None, :, :, None, None]
    diff = b[:, :, None] - b[:, None, :]
    decay = jnp.exp(jnp.where(causal, diff, -jnp.inf))
    A = jnp.einsum('bthk,bshk,btshk->btsh', q, k, decay)
    intra = jnp.einsum('btsh,bshv->bthv', A, v)
    b_last = b[:, -1]
    k_dec = k * jnp.exp(b_last[:, None] - b)
    S_new = jnp.exp(b_last)[..., None] * S + jnp.einsum('bshk,bshv->bhkv', k_dec, v)
    return S_new, inter + intra


def mixer_project(h, w_in, gla_w_alpha, gla_b_alpha, q_norm, k_norm):
    lead = h.shape[:-1]
    offs = np.cumsum(IN_SPLITS)[:-1].tolist()
    gq, gk, gv, gr, ga, sq, sk, sv, gate_a, gate_b = jnp.split(h @ w_in, offs, axis=-1)
    f32 = jnp.float32
    q = gq.astype(f32).reshape(lead + (GLA_HEADS, GLA_DK)) * (GLA_DK ** -0.5)
    k = gk.astype(f32).reshape(lead + (GLA_HEADS, GLA_DK))
    v = gv.astype(f32).reshape(lead + (GLA_HEADS, GLA_DV))
    la = (jax.nn.log_sigmoid((ga @ gla_w_alpha + gla_b_alpha).astype(f32)) / GLA_TAU)
    la = la.reshape(lead + (GLA_HEADS, GLA_DK))
    sq = rms_norm(sq.reshape(lead + (SWA_HEADS, SWA_HD)), q_norm)
    sk = rms_norm(sk.reshape(lead + (SWA_KV_HEADS, SWA_HD)), k_norm)
    sv = sv.reshape(lead + (SWA_KV_HEADS, SWA_HD))
    return q, k, v, la, gr, sq, sk, sv, gate_a, gate_b


def mixer_merge(o_gla, gr, o_swa, gate_a, gate_b, gla_head_norm, w_branch, w_out):
    lead = o_swa.shape[:-1]
    o = rms_norm(o_gla, gla_head_norm).reshape(lead + (GLA_V,))
    o = (o * jax.nn.silu(gr.astype(jnp.float32))).astype(o_swa.dtype)
    y_a = o @ w_branch[:GLA_V]
    y_b = o_swa @ w_branch[GLA_V:]
    m = jax.nn.sigmoid(gate_a) * y_a + jax.nn.sigmoid(gate_b) * y_b
    return m @ w_out


def setup_inputs(seed: int = 0) -> dict:
    key = jax.random.key(seed)
    ks = jax.random.split(key, 24)
    f32 = jnp.float32

    def nrm(k, shape, scale):
        return jax.random.normal(k, shape, f32) * scale

    n_cache = min(WINDOW, PAST_LEN)
    return {
        "x_prompt": nrm(ks[0], (BATCH, SEQ, D_MODEL), 1.0),
        "x_sample": nrm(ks[1], (DEC_BATCH, DEC_SEQ, D_MODEL), 1.0),
        "cache_swa_k": nrm(ks[2], (DEPTH, DEC_BATCH, n_cache, SWA_KV_HEADS, SWA_HD), 1.0),
        "cache_swa_v": nrm(ks[3], (DEPTH, DEC_BATCH, n_cache, SWA_KV_HEADS, SWA_HD), 1.0),
        "state_gla": nrm(ks[4], (DEPTH, DEC_BATCH, GLA_HEADS, GLA_DK, GLA_DV), 0.5),
        "ffn1_norm": 1.0 + nrm(ks[5], (DEPTH, D_MODEL), 0.02),
        "ffn1_w_up": nrm(ks[6], (DEPTH, D_MODEL, 2 * D_FF), D_MODEL ** -0.5),
        "ffn1_w_down": nrm(ks[7], (DEPTH, D_FF, D_MODEL), D_FF ** -0.5),
        "mix_norm": 1.0 + nrm(ks[8], (DEPTH, D_MODEL), 0.02),
        "w_in": nrm(ks[9], (DEPTH, D_MODEL, IN_WIDTH), D_MODEL ** -0.5),
        "gla_w_alpha": nrm(ks[10], (DEPTH, GLA_RANK, GLA_QK), GLA_RANK ** -0.5),
        "gla_b_alpha": nrm(ks[11], (DEPTH, GLA_QK), 0.1),
        "gla_head_norm": 1.0 + nrm(ks[12], (DEPTH, GLA_DV), 0.02),
        "q_norm": 1.0 + nrm(ks[13], (DEPTH, SWA_HD), 0.02),
        "k_norm": 1.0 + nrm(ks[14], (DEPTH, SWA_HD), 0.02),
        "attn_sinks": nrm(ks[15], (DEPTH, SWA_HEADS), 0.5),
        "rel_bias": nrm(ks[16], (N_BUCKETS, SWA_HEADS), 0.1),
        "w_branch": nrm(ks[17], (DEPTH, GLA_V + SWA_Q, D_MODEL), GLA_V ** -0.5),
        "w_out": nrm(ks[18], (DEPTH, D_MODEL, D_MODEL), D_MODEL ** -0.5),
        "ffn2_norm": 1.0 + nrm(ks[19], (DEPTH, D_MODEL), 0.02),
        "ffn2_w_up": nrm(ks[20], (DEPTH, D_MODEL, 2 * D_FF), D_MODEL ** -0.5),
        "ffn2_w_down": nrm(ks[21], (DEPTH, D_FF, D_MODEL), D_FF ** -0.5),
        "final_norm": 1.0 + nrm(ks[22], (DEPTH, D_MODEL), 0.02),
    }


def reference(x_prompt, x_sample, cache_swa_k, cache_swa_v, state_gla,
              ffn1_norm, ffn1_w_up, ffn1_w_down, mix_norm, w_in, gla_w_alpha, gla_b_alpha,
              gla_head_norm, q_norm, k_norm, attn_sinks, rel_bias, w_branch, w_out,
              ffn2_norm, ffn2_w_up, ffn2_w_down, final_norm):
    B, S, _ = x_prompt.shape
    NC = S // CHUNK
    DB, L, _ = x_sample.shape
    n_cache = cache_swa_k.shape[2]

    q_loc = jnp.arange(CHUNK)
    k_loc = jnp.arange(SWA_KEYS) - WIN_CHUNKS * CHUNK
    blk = jnp.arange(NC)[:, None] * CHUNK
    mask_p = window_mask(blk + q_loc[None], blk + k_loc[None])[None, :, None, None]
    bias_p = rel_pos_bias(rel_bias, q_loc, k_loc)
    qpos_s = PAST_LEN + jnp.arange(L)
    kpos_s = jnp.concatenate([PAST_LEN - n_cache + jnp.arange(n_cache), qpos_s])
    mask_s = window_mask(qpos_s, kpos_s)
    bias_s = rel_pos_bias(rel_bias, qpos_s, kpos_s)

    def to_chunks(t):
        return jnp.moveaxis(t.reshape((B, NC, CHUNK) + t.shape[2:]), 1, 0)

    def gla_body(state, inp):
        return gla_chunk(state, *inp)

    xp, xs = x_prompt, x_sample
    kp_l, vp_l, sp_l, ks_l, vs_l, ss_l = [], [], [], [], [], []
    for l in range(DEPTH):
        xp = xp + 0.5 * swiglu_ffn(rms_norm(xp, ffn1_norm[l]), ffn1_w_up[l], ffn1_w_down[l])
        xs = xs + 0.5 * swiglu_ffn(rms_norm(xs, ffn1_norm[l]), ffn1_w_up[l], ffn1_w_down[l])

        q, k, v, la, gr, sq, sk, sv, ga, gb = mixer_project(
            rms_norm(xp, mix_norm[l]), w_in[l], gla_w_alpha[l], gla_b_alpha[l], q_norm[l], k_norm[l])
        S0 = jnp.zeros((B, GLA_HEADS, GLA_DK, GLA_DV), jnp.float32)
        S_fin, o_ch = lax.scan(gla_body, S0, (to_chunks(q), to_chunks(k), to_chunks(v), to_chunks(la)))
        o_gla = jnp.moveaxis(o_ch, 0, 1).reshape(B, S, GLA_HEADS, GLA_DV)
        qb = sq.reshape(B, NC, CHUNK, SWA_HEADS, SWA_HD)
        pad = jnp.zeros((B, WIN_CHUNKS, CHUNK, SWA_KV_HEADS, SWA_HD), sk.dtype)
        kpad = jnp.concatenate([pad, sk.reshape(B, NC, CHUNK, SWA_KV_HEADS, SWA_HD)], axis=1)
        vpad = jnp.concatenate([pad, sv.reshape(B, NC, CHUNK, SWA_KV_HEADS, SWA_HD)], axis=1)
        kb = jnp.concatenate([kpad[:, i:i + NC] for i in range(WIN_CHUNKS + 1)], axis=2)
        vb = jnp.concatenate([vpad[:, i:i + NC] for i in range(WIN_CHUNKS + 1)], axis=2)
        o_swa = sink_attention(qb, kb, vb, bias_p, attn_sinks[l], mask_p).reshape(B, S, SWA_Q)
        xp = xp + mixer_merge(o_gla, gr, o_swa, ga, gb, gla_head_norm[l], w_branch[l], w_out[l])
        n_keep = min(WINDOW, S)
        kp_l.append(sk[:, S - n_keep:])
        vp_l.append(sv[:, S - n_keep:])
        sp_l.append(S_fin.astype(state_gla.dtype))

        q, k, v, la, gr, sq, sk, sv, ga, gb = mixer_project(
            rms_norm(xs, mix_norm[l]), w_in[l], gla_w_alpha[l], gla_b_alpha[l], q_norm[l], k_norm[l])
        S_new, o_gla = gla_chunk(state_gla[l].astype(jnp.float32), q, k, v, la)
        k_all = jnp.concatenate([cache_swa_k[l].astype(sk.dtype), sk], axis=1)
        v_all = jnp.concatenate([cache_swa_v[l].astype(sv.dtype), sv], axis=1)
        o_swa = sink_attention(sq, k_all, v_all, bias_s, attn_sinks[l], mask_s)
        xs = xs + mixer_merge(o_gla, gr, o_swa, ga, gb, gla_head_norm[l], w_branch[l], w_out[l])
        ks_l.append(sk)
        vs_l.append(sv)
        ss_l.append(S_new.astype(state_gla.dtype))

        xp = xp + 0.5 * swiglu_ffn(rms_norm(xp, ffn2_norm[l]), ffn2_w_up[l], ffn2_w_down[l])
        xs = xs + 0.5 * swiglu_ffn(rms_norm(xs, ffn2_norm[l]), ffn2_w_up[l], ffn2_w_down[l])
        xp = rms_norm(xp, final_norm[l])
        xs = rms_norm(xs, final_norm[l])

    y_prompt = xp
    y_sample = xs
    new_k_prompt = jnp.stack(kp_l)
    new_v_prompt = jnp.stack(vp_l)
    new_gla_prompt = jnp.stack(sp_l)
    new_k_sample = jnp.stack(ks_l)
    new_v_sample = jnp.stack(vs_l)
    new_gla_sample = jnp.stack(ss_l)
    return (y_prompt, y_sample, new_k_prompt, new_v_prompt, new_gla_prompt, new_k_sample, new_v_sample, new_gla_sample)
```

```python
import functools
import math

import numpy as np
import jax
import jax.numpy as jnp
from jax import lax
from jax.experimental import pallas as pl
from jax.experimental.pallas import tpu as pltpu

F32 = jnp.float32
BF16 = jnp.bfloat16

D_MODEL = 1024
PAST_LEN = 2048
CHUNK = 64
GLA_HEADS = 4
GLA_DK = 128
GLA_DV = 256
GLA_RANK = 16
GLA_TAU = 16.0
SWA_HEADS = 16
SWA_KV_HEADS = 4
SWA_HD = 64
SWA_GROUP = SWA_HEADS // SWA_KV_HEADS
WINDOW = 128
WIN_CHUNKS = WINDOW // CHUNK
N_BUCKETS = 32
MAX_DISTANCE = 128
D_FF = 2816
EPS = 1e-6
GLA_QK = GLA_HEADS * GLA_DK
GLA_V = GLA_HEADS * GLA_DV
SWA_Q = SWA_HEADS * SWA_HD
SWA_KV = SWA_KV_HEADS * SWA_HD

LANES = 128
RANK_PAD = LANES
VMEM_LIMIT = 56 * 1024 * 1024
ROW_TILE = 512
GLA_BLOCK = 128
SWA_QBLOCK = 2 * CHUNK
SWA_KBLOCK = SWA_QBLOCK + WINDOW


def _dot(a, b):
    return jnp.dot(a, b, preferred_element_type=F32)


def _dot_nt(a, b):
    return lax.dot_general(a, b, (((1,), (1,)), ((), ())), preferred_element_type=F32)


def _dot_tn(a, b):
    return lax.dot_general(a, b, (((0,), (0,)), ((), ())), preferred_element_type=F32)


def _rms(x):
    return x * lax.rsqrt(jnp.mean(x * x, axis=-1, keepdims=True) + EPS)


def _resident(shape):
    nd = len(shape)
    return pl.BlockSpec(shape, lambda *_: (0,) * nd, pipeline_mode=pl.Buffered(1))


def _params(sem):
    return pltpu.CompilerParams(dimension_semantics=sem, vmem_limit_bytes=VMEM_LIMIT)


def _ffn_kernel(x_ref, g_ref, wa_ref, wb_ref, wd_ref, fg_ref, o_ref, *, final_norm):
    x = x_ref[...]
    h = (_rms(x) * g_ref[...]).astype(BF16)
    a = _dot(h, wa_ref[...])
    b = _dot(h, wb_ref[...])
    act = (a * jax.nn.sigmoid(a) * b).astype(BF16)
    y = x + 0.5 * _dot(act, wd_ref[...])
    if final_norm:
        y = _rms(y) * fg_ref[...]
    o_ref[...] = y


def _ffn(x, g, wa, wb, wd, fg, *, final_norm, name):
    t = x.shape[0]
    tm = min(ROW_TILE, t)
    row = pl.BlockSpec((tm, D_MODEL), lambda i: (i, 0))
    return pl.pallas_call(
        functools.partial(_ffn_kernel, final_norm=final_norm),
        grid=(t // tm,),
        in_specs=[row, _resident(g.shape), _resident(wa.shape), _resident(wb.shape),
                  _resident(wd.shape), _resident(fg.shape)],
        out_specs=row,
        out_shape=jax.ShapeDtypeStruct((t, D_MODEL), F32),
        compiler_params=_params(("parallel",)),
        name=name,
    )(x, g, wa, wb, wd, fg)


def _decay_matrix(tm, chunk):
    nc = tm // chunk
    mid = chunk // 2 - 1
    m = np.zeros((tm + 2 * nc, tm), np.float32)
    for c in range(nc):
        o = c * chunk
        for t in range(chunk):
            if t > mid:
                m[o + t, o + mid + 1:o + t + 1] = 1.0
            else:
                m[o + t, o + t + 1:o + mid + 1] = -1.0
        m[tm + c, o:o + mid + 1] = 1.0
        m[tm + nc + c, o + mid + 1:o + chunk] = 1.0
    return m


def _proj_kernel(x_ref, g_ref, wq_ref, wk_ref, wv_ref, wr_ref, wa_ref, wal_ref, bal_ref,
                 wsq_ref, wsk_ref, wsv_ref, wga_ref, wgb_ref, gq_ref, gk_ref, qn_ref, kn_ref, dm_ref,
                 qt_ref, kt_ref, v_ref, sgr_ref, e_ref, sq_ref, sk_ref, sv_ref, siga_ref, sigb_ref, *, tm):
    h = (_rms(x_ref[...]) * g_ref[...]).astype(BF16)

    ga = _dot(h, wa_ref[...])
    z = _dot(ga.astype(BF16), wal_ref[...]) + bal_ref[...]
    la = (jnp.minimum(z, 0.0) - jnp.log1p(jnp.exp(-jnp.abs(z)))) * (1.0 / GLA_TAU)
    la_hi = la.astype(BF16)
    la_lo = (la - la_hi.astype(F32)).astype(BF16)
    dd = _dot(dm_ref[...], la_hi) + _dot(dm_ref[...], la_lo)
    d = dd[:tm]
    e_ref[0] = jnp.exp(dd[tm:])
    q = _dot(h, wq_ref[...]) * (GLA_DK ** -0.5)
    qt_ref[...] = (q * jnp.exp(d)).astype(BF16)
    k = _dot(h, wk_ref[...])
    kt_ref[...] = (k * jnp.exp(-d)).astype(BF16)
    v_ref[...] = _dot(h, wv_ref[...]).astype(BF16)
    r = _dot(h, wr_ref[...])
    sgr_ref[...] = (r * jax.nn.sigmoid(r)).astype(BF16)

    uq = _dot(h, wsq_ref[...])
    ssq = _dot((uq * uq).astype(BF16), gq_ref[...])
    sqn = uq * lax.rsqrt(ssq * (1.0 / SWA_HD) + EPS) * qn_ref[...]
    sq_ref[...] = (sqn * (SWA_HD ** -0.5)).astype(BF16)
    uk = _dot(h, wsk_ref[...])
    ssk = _dot((uk * uk).astype(BF16), gk_ref[...])
    sk_ref[...] = uk * lax.rsqrt(ssk * (1.0 / SWA_HD) + EPS) * kn_ref[...]
    sv_ref[...] = _dot(h, wsv_ref[...])

    siga_ref[...] = jax.nn.sigmoid(_dot(h, wga_ref[...])).astype(BF16)
    sigb_ref[...] = jax.nn.sigmoid(_dot(h, wgb_ref[...])).astype(BF16)


def _proj(x, w, *, chunk, name):
    t = x.shape[0]
    tm = min(ROW_TILE, t)
    nt = t // tm
    nc = tm // chunk
    dm = jnp.asarray(_decay_matrix(tm, chunk), BF16)
    ins = [w["mix_norm"], w["wq"], w["wk"], w["wv"], w["wr"], w["wa"], w["walpha"], w["balpha"],
           w["wsq"], w["wsk"], w["wsv"], w["wga"], w["wgb"], w["gq"], w["gk"], w["qn"], w["kn"], dm]

    def row(n):
        return pl.BlockSpec((tm, n), lambda i: (i, 0))

    outs = [
        (GLA_QK, BF16), (GLA_QK, BF16), (GLA_V, BF16), (GLA_V, BF16), None,
        (SWA_Q, BF16), (SWA_KV, F32), (SWA_KV, F32), (D_MODEL, BF16), (D_MODEL, BF16),
    ]
    out_shape, out_specs = [], []
    for o in outs:
        if o is None:
            out_shape.append(jax.ShapeDtypeStruct((nt, 2 * nc, GLA_QK), F32))
            out_specs.append(pl.BlockSpec((1, 2 * nc, GLA_QK), lambda i: (i, 0, 0)))
        else:
            out_shape.append(jax.ShapeDtypeStruct((t, o[0]), o[1]))
            out_specs.append(row(o[0]))
    res = list(pl.pallas_call(
        functools.partial(_proj_kernel, tm=tm),
        grid=(nt,),
        in_specs=[row(D_MODEL)] + [_resident(a.shape) for a in ins],
        out_specs=out_specs,
        out_shape=out_shape,
        compiler_params=_params(("parallel",)),
        name=name,
    )(x, *ins))
    res[4] = jnp.transpose(res[4].reshape(nt, 2, nc, GLA_QK), (1, 0, 2, 3)).reshape(2, nt * nc, 1, GLA_QK)
    return res


def _gla_kernel(qt_ref, kt_ref, v_ref, sgr_ref, e_ref, s0_ref, hn_ref, o_ref, sfin_ref, st_ref, *, nb, blk):
    c = pl.program_id(0)

    @pl.when(c == 0)
    def _():
        for b in range(nb):
            for h in range(GLA_HEADS):
                st_ref[b, h] = s0_ref[b, h].T

    rows = lax.broadcasted_iota(jnp.int32, (blk, blk), 0)
    cols = lax.broadcasted_iota(jnp.int32, (blk, blk), 1)
    causal = rows >= cols
    hn = hn_ref[...]
    for b in range(nb):
        for h in range(GLA_HEADS):
            ks = slice(h * GLA_DK, (h + 1) * GLA_DK)
            vs = slice(h * GLA_DV, (h + 1) * GLA_DV)
            qb = qt_ref[b, :, ks]
            kb = kt_ref[b, :, ks]
            vb = v_ref[b, :, vs]
            e1 = e_ref[0, b, 0, :, ks]
            e2 = e_ref[1, b, 0, :, ks]
            s_t = st_ref[b, h] * e1
            a = jnp.where(causal, _dot_nt(qb, kb), 0.0).astype(BF16)
            o = _dot_nt(qb, s_t.astype(BF16)) + _dot(a, vb)
            st_ref[b, h] = (s_t + _dot_tn(vb, kb)) * e2
            y = _rms(o) * hn * sgr_ref[b, :, vs].astype(F32)
            o_ref[b, :, vs] = y.astype(BF16)

    @pl.when(c == pl.num_programs(0) - 1)
    def _():
        for b in range(nb):
            for h in range(GLA_HEADS):
                sfin_ref[b, h] = st_ref[b, h].T


def _gla(qt, kt, v, sgr, e, s0, hn, *, blk, name):
    nb, s, _ = qt.shape
    nc = s // blk

    def seq(n):
        return pl.BlockSpec((nb, blk, n), lambda c: (0, c, 0))

    return pl.pallas_call(
        functools.partial(_gla_kernel, nb=nb, blk=blk),
        grid=(nc,),
        in_specs=[seq(GLA_QK), seq(GLA_QK), seq(GLA_V), seq(GLA_V),
                  pl.BlockSpec((2, nb, 1, 1, GLA_QK), lambda c: (0, 0, c, 0, 0)),
                  _resident(s0.shape), _resident(hn.shape)],
        out_specs=[seq(GLA_V), pl.BlockSpec(s0.shape, lambda c: (0, 0, 0, 0))],
        out_shape=[jax.ShapeDtypeStruct((nb, s, GLA_V), BF16), jax.ShapeDtypeStruct(s0.shape, F32)],
        scratch_shapes=[pltpu.VMEM((nb, GLA_HEADS, GLA_DV, GLA_DK), F32)],
        compiler_params=_params(("arbitrary",)),
        name=name,
    )(qt, kt, v, sgr, e, s0, hn)


def _t5_bucket(rel):
    nb = N_BUCKETS // 2
    ret = jnp.where(rel > 0, nb, 0)
    n = jnp.abs(rel)
    max_exact = nb // 2
    nf = jnp.maximum(n, 1).astype(jnp.float32)
    large = max_exact + (jnp.log(nf / max_exact) / math.log(MAX_DISTANCE / max_exact)
                         * (nb - max_exact)).astype(jnp.int32)
    large = jnp.minimum(large, nb - 1)
    return ret + jnp.where(n < max_exact, n, large)


def _bias_kernel(bucket_ref, rb_ref, o_ref):
    bucket = bucket_ref[...]
    for h in range(SWA_HEADS):
        acc = jnp.where(bucket < 0, -jnp.inf, 0.0).astype(F32)
        for i in range(N_BUCKETS):
            acc = jnp.where(bucket == i, rb_ref[i, h], acc)
        o_ref[h] = acc


def _bias_table(bucket, rel_bias, *, name):
    tq, tk = bucket.shape
    return pl.pallas_call(
        _bias_kernel,
        in_specs=[pl.BlockSpec((tq, tk), lambda: (0, 0)), pl.BlockSpec(memory_space=pltpu.SMEM)],
        out_specs=pl.BlockSpec((SWA_HEADS, tq, tk), lambda: (0, 0, 0)),
        out_shape=jax.ShapeDtypeStruct((SWA_HEADS, tq, tk), F32),
        name=name,
    )(bucket, rel_bias)


def _attend(q_ref_row, k, v, bias_ref, sink_ref, extra_mask):
    outs = []
    for h in range(SWA_HEADS):
        g = h // SWA_GROUP
        gs = slice(g * SWA_HD, (g + 1) * SWA_HD)
        s = _dot_nt(q_ref_row(h), k[:, gs]) + bias_ref[h]
        if extra_mask is not None:
            s = jnp.where(extra_mask, -jnp.inf, s)
        sink = sink_ref[h]
        m = jnp.maximum(jnp.max(s, axis=-1, keepdims=True), sink)
        p = jnp.exp(s - m)
        denom = jnp.sum(p, axis=-1, keepdims=True) + jnp.exp(sink - m)
        probs = (p / denom).astype(BF16)
        outs.append(_dot(probs, v[:, gs]))
    return jnp.concatenate(outs, axis=-1)


def _swa_prompt_kernel(q_ref, kp_ref, kc_ref, vp_ref, vc_ref, bias_ref, sink_ref, o_ref):
    j = pl.program_id(1)
    k = jnp.concatenate([kp_ref[0], kc_ref[0]], axis=0).astype(BF16)
    v = jnp.concatenate([vp_ref[0], vc_ref[0]], axis=0).astype(BF16)
    cols = lax.broadcasted_iota(jnp.int32, (SWA_QBLOCK, SWA_KBLOCK), 1)
    before_start = jnp.logical_and(j == 0, cols < WINDOW)
    out = _attend(lambda h: q_ref[0, :, h * SWA_HD:(h + 1) * SWA_HD], k, v, bias_ref, sink_ref, before_start)
    o_ref[0] = out.astype(BF16)


def _swa_prompt(sq, sk, sv, bias, sinks, *, name):
    nb, s, _ = sq.shape
    nj = s // SWA_QBLOCK
    cur = lambda n: pl.BlockSpec((1, SWA_QBLOCK, n), lambda b, j: (b, j, 0))
    prev = lambda n: pl.BlockSpec((1, WINDOW, n), lambda b, j: (b, jnp.maximum(j - 1, 0), 0))
    return pl.pallas_call(
        _swa_prompt_kernel,
        grid=(nb, nj),
        in_specs=[cur(SWA_Q), prev(SWA_KV), cur(SWA_KV), prev(SWA_KV), cur(SWA_KV),
                  _resident(bias.shape), pl.BlockSpec(memory_space=pltpu.SMEM)],
        out_specs=cur(SWA_Q),
        out_shape=jax.ShapeDtypeStruct((nb, s, SWA_Q), BF16),
        compiler_params=_params(("parallel", "parallel")),
        name=name,
    )(sq, sk, sk, sv, sv, bias, sinks)


def _swa_sample_kernel(q_ref, ck_ref, nk_ref, cv_ref, nv_ref, bias_ref, sink_ref, o_ref):
    k = jnp.concatenate([ck_ref[0], nk_ref[0]], axis=0).astype(BF16)
    v = jnp.concatenate([cv_ref[0], nv_ref[0]], axis=0).astype(BF16)
    out = _attend(lambda h: q_ref[0, :, h * SWA_HD:(h + 1) * SWA_HD], k, v, bias_ref, sink_ref, None)
    o_ref[0] = out.astype(BF16)


def _swa_sample(sq, ck, nk, cv, nv, bias, sinks, *, name):
    nb, l, _ = sq.shape
    n_cache = ck.shape[1]
    blk = lambda r, n: pl.BlockSpec((1, r, n), lambda b: (b, 0, 0))
    return pl.pallas_call(
        _swa_sample_kernel,
        grid=(nb,),
        in_specs=[blk(l, SWA_Q), blk(n_cache, SWA_KV), blk(l, SWA_KV), blk(n_cache, SWA_KV), blk(l, SWA_KV),
                  _resident(bias.shape), pl.BlockSpec(memory_space=pltpu.SMEM)],
        out_specs=blk(l, SWA_Q),
        out_shape=jax.ShapeDtypeStruct((nb, l, SWA_Q), BF16),
        compiler_params=_params(("parallel",)),
        name=name,
    )(sq, ck, nk, cv, nv, bias, sinks)


def _merge_kernel(x_ref, og_ref, os_ref, siga_ref, sigb_ref, wba_ref, wbb_ref, wo_ref, o_ref):
    ya = _dot(og_ref[...], wba_ref[...])
    yb = _dot(os_ref[...], wbb_ref[...])
    m = (siga_ref[...].astype(F32) * ya + sigb_ref[...].astype(F32) * yb).astype(BF16)
    o_ref[...] = x_ref[...] + _dot(m, wo_ref[...])


def _merge(x, og, osw, siga, sigb, wba, wbb, wo, *, name):
    t = x.shape[0]
    tm = min(ROW_TILE, t)
    row = pl.BlockSpec((tm, D_MODEL), lambda i: (i, 0))
    return pl.pallas_call(
        _merge_kernel,
        grid=(t // tm,),
        in_specs=[row] * 5 + [_resident(wba.shape), _resident(wbb.shape), _resident(wo.shape)],
        out_specs=row,
        out_shape=jax.ShapeDtypeStruct((t, D_MODEL), F32),
        compiler_params=_params(("parallel",)),
        name=name,
    )(x, og, osw, siga, sigb, wba, wbb, wo)


def _layer_weights(l, ffn1_norm, ffn1_w_up, ffn1_w_down, mix_norm, w_in, gla_w_alpha, gla_b_alpha,
                   gla_head_norm, q_norm, k_norm, w_branch, w_out, ffn2_norm, ffn2_w_up, ffn2_w_down, final_norm):
    splits = (GLA_QK, GLA_QK, GLA_V, GLA_V, GLA_RANK, SWA_Q, SWA_KV, SWA_KV, D_MODEL, D_MODEL)
    offs = np.concatenate([[0], np.cumsum(splits)])
    wi = w_in[l]
    cols = [wi[:, offs[i]:offs[i + 1]].astype(BF16) for i in range(len(splits))]
    hd_group = np.kron(np.eye(SWA_HEADS, dtype=np.float32), np.ones((SWA_HD, SWA_HD), np.float32))
    w = {
        "mix_norm": mix_norm[l][None],
        "wq": cols[0], "wk": cols[1], "wv": cols[2], "wr": cols[3],
        "wa": jnp.pad(cols[4], ((0, 0), (0, RANK_PAD - GLA_RANK))),
        "walpha": jnp.pad(gla_w_alpha[l].astype(BF16), ((0, RANK_PAD - GLA_RANK), (0, 0))),
        "balpha": gla_b_alpha[l][None],
        "wsq": cols[5], "wsk": cols[6], "wsv": cols[7], "wga": cols[8], "wgb": cols[9],
        "gq": jnp.asarray(hd_group, BF16),
        "gk": jnp.asarray(hd_group[:SWA_KV, :SWA_KV], BF16),
        "qn": jnp.tile(q_norm[l], SWA_HEADS)[None],
        "kn": jnp.tile(k_norm[l], SWA_KV_HEADS)[None],
    }
    ffn1 = (ffn1_norm[l][None], ffn1_w_up[l][:, :D_FF].astype(BF16), ffn1_w_up[l][:, D_FF:].astype(BF16),
            ffn1_w_down[l].astype(BF16))
    ffn2 = (ffn2_norm[l][None], ffn2_w_up[l][:, :D_FF].astype(BF16), ffn2_w_up[l][:, D_FF:].astype(BF16),
            ffn2_w_down[l].astype(BF16))
    merge = (w_branch[l][:GLA_V].astype(BF16), w_branch[l][GLA_V:].astype(BF16), w_out[l].astype(BF16))
    return w, ffn1, ffn2, merge, gla_head_norm[l][None], final_norm[l][None]


def kernel(x_prompt, x_sample, cache_swa_k, cache_swa_v, state_gla, ffn1_norm, ffn1_w_up, ffn1_w_down, mix_norm, w_in, gla_w_alpha, gla_b_alpha, gla_head_norm, q_norm, k_norm, attn_sinks, rel_bias, w_branch, w_out, ffn2_norm, ffn2_w_up, ffn2_w_down, final_norm):
    nbp, s, _ = x_prompt.shape
    nbs, l, _ = x_sample.shape
    depth = ffn1_norm.shape[0]
    n_cache = cache_swa_k.shape[2]
    past = PAST_LEN
    assert s % SWA_QBLOCK == 0 and s % GLA_BLOCK == 0 and n_cache == WINDOW and s >= WINDOW

    tq = jnp.arange(SWA_QBLOCK)
    tk = jnp.arange(SWA_KBLOCK) - WINDOW
    qc, kc = tq // CHUNK, tk // CHUNK
    vis_p = (kc[None, :] <= qc[:, None]) & (kc[None, :] >= qc[:, None] - WIN_CHUNKS)
    bucket_p = jnp.where(vis_p, _t5_bucket(tk[None, :] - tq[:, None]), -1).astype(jnp.int32)
    qpos_s = past + jnp.arange(l)
    kpos_s = jnp.concatenate([past - n_cache + jnp.arange(n_cache), qpos_s])
    qcs, kcs = qpos_s // CHUNK, kpos_s // CHUNK
    vis_s = (kpos_s[None, :] >= 0) & (kcs[None, :] <= qcs[:, None]) & (kcs[None, :] >= qcs[:, None] - WIN_CHUNKS)
    bucket_s = jnp.where(vis_s, _t5_bucket(kpos_s[None, :] - qpos_s[:, None]), -1).astype(jnp.int32)
    bias_p = _bias_table(bucket_p, rel_bias, name="bias_prompt")
    bias_s = _bias_table(bucket_s, rel_bias, name="bias_sample")

    xp = x_prompt.reshape(nbp * s, D_MODEL)
    xs = x_sample.reshape(nbs * l, D_MODEL)
    outs = [[] for _ in range(6)]
    for layer in range(depth):
        w, ffn1, ffn2, mrg, hn, fn = _layer_weights(
            layer, ffn1_norm, ffn1_w_up, ffn1_w_down, mix_norm, w_in, gla_w_alpha, gla_b_alpha,
            gla_head_norm, q_norm, k_norm, w_branch, w_out, ffn2_norm, ffn2_w_up, ffn2_w_down, final_norm)
        sinks = attn_sinks[layer]

        xp = _ffn(xp, *ffn1, fn, final_norm=False, name="ffn1_prompt")
        qt, kt, v, sgr, e, sq, sk, sv, siga, sigb = _proj(xp, w, chunk=GLA_BLOCK, name="proj_prompt")
        e = e.reshape(2, nbp, s // GLA_BLOCK, 1, GLA_QK)
        r3 = lambda a: a.reshape(nbp, s, a.shape[-1])
        og, s_fin = _gla(r3(qt), r3(kt), r3(v), r3(sgr), e,
                         jnp.zeros((nbp, GLA_HEADS, GLA_DK, GLA_DV), F32), hn, blk=GLA_BLOCK, name="gla_prompt")
        osw = _swa_prompt(r3(sq), r3(sk), r3(sv), bias_p, sinks, name="swa_prompt")
        xp = _merge(xp, og.reshape(nbp * s, GLA_V), osw.reshape(nbp * s, SWA_Q), siga, sigb, *mrg,
                    name="merge_prompt")
        n_keep = min(WINDOW, s)
        outs[0].append(r3(sk)[:, s - n_keep:].reshape(nbp, n_keep, SWA_KV_HEADS, SWA_HD))
        outs[1].append(r3(sv)[:, s - n_keep:].reshape(nbp, n_keep, SWA_KV_HEADS, SWA_HD))
        outs[2].append(s_fin)
        xp = _ffn(xp, *ffn2, fn, final_norm=True, name="ffn2_prompt")

        xs = _ffn(xs, *ffn1, fn, final_norm=False, name="ffn1_sample")
        qt, kt, v, sgr, e, sq, sk, sv, siga, sigb = _proj(xs, w, chunk=l, name="proj_sample")
        e = e.reshape(2, nbs, 1, 1, GLA_QK)
        r3 = lambda a: a.reshape(nbs, l, a.shape[-1])
        og, s_new = _gla(r3(qt), r3(kt), r3(v), r3(sgr), e, state_gla[layer], hn, blk=l, name="gla_sample")
        ck = cache_swa_k[layer].reshape(nbs, n_cache, SWA_KV)
        cv = cache_swa_v[layer].reshape(nbs, n_cache, SWA_KV)
        osw = _swa_sample(r3(sq), ck, r3(sk), cv, r3(sv), bias_s, sinks, name="swa_sample")
        xs = _merge(xs, og.reshape(nbs * l, GLA_V), osw.reshape(nbs * l, SWA_Q), siga, sigb, *mrg,
                    name="merge_sample")
        outs[3].append(r3(sk).reshape(nbs, l, SWA_KV_HEADS, SWA_HD))
        outs[4].append(r3(sv).reshape(nbs, l, SWA_KV_HEADS, SWA_HD))
        outs[5].append(s_new)
        xs = _ffn(xs, *ffn2, fn, final_norm=True, name="ffn2_sample")

    return (xp.reshape(nbp, s, D_MODEL), xs.reshape(nbs, l, D_MODEL),
            jnp.stack(outs[0]), jnp.stack(outs[1]), jnp.stack(outs[2]),
            jnp.stack(outs[3]), jnp.stack(outs[4]), jnp.stack(outs[5]))
```

```python
import functools
import math

import numpy as np
import jax
import jax.numpy as jnp
from jax import lax
from jax.experimental import pallas as pl
from jax.experimental.pallas import tpu as pltpu

F32 = jnp.float32
BF16 = jnp.bfloat16

D_MODEL = 1024
PAST_LEN = 2048
CHUNK = 64
GLA_HEADS = 4
GLA_DK = 128
GLA_DV = 256
GLA_RANK = 16
GLA_TAU = 16.0
SWA_HEADS = 16
SWA_KV_HEADS = 4
SWA_HD = 64
SWA_GROUP = SWA_HEADS // SWA_KV_HEADS
WINDOW = 128
WIN_CHUNKS = WINDOW // CHUNK
N_BUCKETS = 32
MAX_DISTANCE = 128
D_FF = 2816
EPS = 1e-6
LOG2E = 1.4426950408889634
GLA_QK = GLA_HEADS * GLA_DK
GLA_V = GLA_HEADS * GLA_DV
SWA_Q = SWA_HEADS * SWA_HD
SWA_KV = SWA_KV_HEADS * SWA_HD

LANES = 128
RANK_PAD = LANES
VMEM_LIMIT = 56 * 1024 * 1024
ROW_TILE = 512
GLA_BLOCK = 128
SWA_QBLOCK = 2 * CHUNK
SWA_KBLOCK = SWA_QBLOCK + WINDOW


def _dot(a, b):
    return jnp.dot(a, b, preferred_element_type=F32)


def _dot_nt(a, b):
    return lax.dot_general(a, b, (((1,), (1,)), ((), ())), preferred_element_type=F32)


def _dot_tn(a, b):
    return lax.dot_general(a, b, (((0,), (0,)), ((), ())), preferred_element_type=F32)


def _rms(x):
    return x * lax.rsqrt(jnp.mean(x * x, axis=-1, keepdims=True) + EPS)


def _resident(shape):
    nd = len(shape)
    return pl.BlockSpec(shape, lambda *_: (0,) * nd, pipeline_mode=pl.Buffered(1))


def _params(sem):
    return pltpu.CompilerParams(dimension_semantics=sem, vmem_limit_bytes=VMEM_LIMIT)


def _ffn_kernel(x_ref, g_ref, wa_ref, wb_ref, wd_ref, fg_ref, o_ref, *, final_norm):
    x = x_ref[...]
    h = (_rms(x) * g_ref[...]).astype(BF16)
    a = _dot(h, wa_ref[...])
    b = _dot(h, wb_ref[...])
    act = (a * jax.nn.sigmoid(a) * b).astype(BF16)
    y = x + 0.5 * _dot(act, wd_ref[...])
    if final_norm:
        y = _rms(y) * fg_ref[...]
    o_ref[...] = y


def _ffn(x, g, wa, wb, wd, fg, *, final_norm, name):
    t = x.shape[0]
    tm = min(ROW_TILE, t)
    row = pl.BlockSpec((tm, D_MODEL), lambda i: (i, 0))
    return pl.pallas_call(
        functools.partial(_ffn_kernel, final_norm=final_norm),
        grid=(t // tm,),
        in_specs=[row, _resident(g.shape), _resident(wa.shape), _resident(wb.shape),
                  _resident(wd.shape), _resident(fg.shape)],
        out_specs=row,
        out_shape=jax.ShapeDtypeStruct((t, D_MODEL), F32),
        compiler_params=_params(("parallel",)),
        name=name,
    )(x, g, wa, wb, wd, fg)


def _decay_matrix(tm, chunk):
    nc = tm // chunk
    mid = chunk // 2 - 1
    m = np.zeros((tm + 2 * nc, tm), np.float32)
    for c in range(nc):
        o = c * chunk
        for t in range(chunk):
            if t > mid:
                m[o + t, o + mid + 1:o + t + 1] = 1.0
            else:
                m[o + t, o + t + 1:o + mid + 1] = -1.0
        m[tm + c, o:o + mid + 1] = 1.0
        m[tm + nc + c, o + mid + 1:o + chunk] = 1.0
    return m


def _proj_kernel(x_ref, g_ref, wq_ref, wk_ref, wv_ref, wr_ref, wa_ref, wal_ref, bal_ref,
                 wsq_ref, wsk_ref, wsv_ref, wga_ref, wgb_ref, gq_ref, gk_ref, qn_ref, kn_ref, dm_ref,
                 qt_ref, kt_ref, v_ref, sgr_ref, e_ref, sq_ref, sk_ref, sv_ref, siga_ref, sigb_ref,
                 kx_ref, svt_ref, *, tm):
    h = (_rms(x_ref[...]) * g_ref[...]).astype(BF16)

    ga = _dot(h, wa_ref[...])
    z = _dot(ga.astype(BF16), wal_ref[...]) + bal_ref[...]
    la = (jnp.minimum(z, 0.0) - jnp.log1p(jnp.exp(-jnp.abs(z)))) * (1.0 / GLA_TAU)
    la_hi = la.astype(BF16)
    la_lo = (la - la_hi.astype(F32)).astype(BF16)
    dd = _dot(dm_ref[...], la_hi) + _dot(dm_ref[...], la_lo)
    d = dd[:tm]
    e_ref[0] = jnp.exp(dd[tm:])
    q = _dot(h, wq_ref[...]) * (GLA_DK ** -0.5)
    qt_ref[...] = (q * jnp.exp(d)).astype(BF16)
    k = _dot(h, wk_ref[...])
    kt_ref[...] = (k * jnp.exp(-d)).astype(BF16)
    v_ref[...] = _dot(h, wv_ref[...]).astype(BF16)
    r = _dot(h, wr_ref[...])
    sgr_ref[...] = (r * jax.nn.sigmoid(r)).astype(BF16)

    uq = _dot(h, wsq_ref[...])
    ssq = _dot((uq * uq).astype(BF16), gq_ref[...])
    sqn = uq * lax.rsqrt(ssq * (1.0 / SWA_HD) + EPS) * qn_ref[...]
    sq_ref[...] = (sqn * (SWA_HD ** -0.5 * LOG2E)).astype(BF16)
    uk = _dot(h, wsk_ref[...])
    ssk = _dot((uk * uk).astype(BF16), gk_ref[...])
    skn = uk * lax.rsqrt(ssk * (1.0 / SWA_HD) + EPS) * kn_ref[...]
    sk_ref[...] = skn
    uv = _dot(h, wsv_ref[...])
    sv_ref[...] = uv
    kx_ref[...] = jnp.concatenate(
        [skn[:, g * SWA_HD:(g + 1) * SWA_HD] for g in range(SWA_KV_HEADS) for _ in range(SWA_GROUP)],
        axis=1).astype(BF16)
    svt_ref[...] = uv.T.astype(BF16)

    siga_ref[...] = jax.nn.sigmoid(_dot(h, wga_ref[...])).astype(BF16)
    sigb_ref[...] = jax.nn.sigmoid(_dot(h, wgb_ref[...])).astype(BF16)


def _proj(x, w, *, chunk, name):
    t = x.shape[0]
    tm = min(ROW_TILE, t)
    nt = t // tm
    nc = tm // chunk
    dm = jnp.asarray(_decay_matrix(tm, chunk), BF16)
    ins = [w["mix_norm"], w["wq"], w["wk"], w["wv"], w["wr"], w["wa"], w["walpha"], w["balpha"],
           w["wsq"], w["wsk"], w["wsv"], w["wga"], w["wgb"], w["gq"], w["gk"], w["qn"], w["kn"], dm]

    def row(n):
        return pl.BlockSpec((tm, n), lambda i: (i, 0))

    outs = [
        (GLA_QK, BF16), (GLA_QK, BF16), (GLA_V, BF16), (GLA_V, BF16), None,
        (SWA_Q, BF16), (SWA_KV, F32), (SWA_KV, F32), (D_MODEL, BF16), (D_MODEL, BF16), (SWA_Q, BF16),
    ]
    out_shape, out_specs = [], []
    for o in outs:
        if o is None:
            out_shape.append(jax.ShapeDtypeStruct((nt, 2 * nc, GLA_QK), F32))
            out_specs.append(pl.BlockSpec((1, 2 * nc, GLA_QK), lambda i: (i, 0, 0)))
        else:
            out_shape.append(jax.ShapeDtypeStruct((t, o[0]), o[1]))
            out_specs.append(row(o[0]))
    out_shape.append(jax.ShapeDtypeStruct((SWA_KV, t), BF16))
    out_specs.append(pl.BlockSpec((SWA_KV, tm), lambda i: (0, i)))
    res = list(pl.pallas_call(
        functools.partial(_proj_kernel, tm=tm),
        grid=(nt,),
        in_specs=[row(D_MODEL)] + [_resident(a.shape) for a in ins],
        out_specs=out_specs,
        out_shape=out_shape,
        compiler_params=_params(("parallel",)),
        name=name,
    )(x, *ins))
    res[4] = jnp.transpose(res[4].reshape(nt, 2, nc, GLA_QK), (1, 0, 2, 3)).reshape(2, nt * nc, 1, GLA_QK)
    return res


def _gla_kernel(qt_ref, kt_ref, v_ref, sgr_ref, e_ref, s0_ref, hn_ref, o_ref, sfin_ref, st_ref, *, nb, blk):
    c = pl.program_id(0)

    @pl.when(c == 0)
    def _():
        for b in range(nb):
            for h in range(GLA_HEADS):
                st_ref[b, h] = s0_ref[b, h].T

    rows = lax.broadcasted_iota(jnp.int32, (blk, blk), 0)
    cols = lax.broadcasted_iota(jnp.int32, (blk, blk), 1)
    causal = rows >= cols
    hn = hn_ref[...]
    for b in range(nb):
        for h in range(GLA_HEADS):
            ks = slice(h * GLA_DK, (h + 1) * GLA_DK)
            vs = slice(h * GLA_DV, (h + 1) * GLA_DV)
            qb = qt_ref[b, :, ks]
            kb = kt_ref[b, :, ks]
            vb = v_ref[b, :, vs]
            e1 = e_ref[0, b, 0, :, ks]
            e2 = e_ref[1, b, 0, :, ks]
            s_t = st_ref[b, h] * e1
            a = jnp.where(causal, _dot_nt(qb, kb), 0.0).astype(BF16)
            o = _dot_nt(qb, s_t.astype(BF16)) + _dot(a, vb)
            st_ref[b, h] = (s_t + _dot_tn(vb, kb)) * e2
            y = _rms(o) * hn * sgr_ref[b, :, vs].astype(F32)
            o_ref[b, :, vs] = y.astype(BF16)

    @pl.when(c == pl.num_programs(0) - 1)
    def _():
        for b in range(nb):
            for h in range(GLA_HEADS):
                sfin_ref[b, h] = st_ref[b, h].T


def _gla(qt, kt, v, sgr, e, s0, hn, *, blk, name):
    nb, s, _ = qt.shape
    nc = s // blk

    def seq(n):
        return pl.BlockSpec((nb, blk, n), lambda c: (0, c, 0))

    return pl.pallas_call(
        functools.partial(_gla_kernel, nb=nb, blk=blk),
        grid=(nc,),
        in_specs=[seq(GLA_QK), seq(GLA_QK), seq(GLA_V), seq(GLA_V),
                  pl.BlockSpec((2, nb, 1, 1, GLA_QK), lambda c: (0, 0, c, 0, 0)),
                  _resident(s0.shape), _resident(hn.shape)],
        out_specs=[seq(GLA_V), pl.BlockSpec(s0.shape, lambda c: (0, 0, 0, 0))],
        out_shape=[jax.ShapeDtypeStruct((nb, s, GLA_V), BF16), jax.ShapeDtypeStruct(s0.shape, F32)],
        scratch_shapes=[pltpu.VMEM((nb, GLA_HEADS, GLA_DV, GLA_DK), F32)],
        compiler_params=_params(("arbitrary",)),
        name=name,
    )(qt, kt, v, sgr, e, s0, hn)


def _t5_bucket(rel):
    nb = N_BUCKETS // 2
    ret = jnp.where(rel > 0, nb, 0)
    n = jnp.abs(rel)
    max_exact = nb // 2
    nf = jnp.maximum(n, 1).astype(jnp.float32)
    large = max_exact + (jnp.log(nf / max_exact) / math.log(MAX_DISTANCE / max_exact)
                         * (nb - max_exact)).astype(jnp.int32)
    large = jnp.minimum(large, nb - 1)
    return ret + jnp.where(n < max_exact, n, large)


def _bias_select(bucket, rb_ref, h):
    acc = jnp.where(bucket < 0, -jnp.inf, 0.0).astype(F32)
    for i in range(N_BUCKETS):
        acc = jnp.where(bucket == i, rb_ref[i, h] * LOG2E, acc)
    return acc


def _bias_cols_kernel(bucket_ref, rb_ref, o_ref):
    nk, nq = bucket_ref.shape
    rows = 16
    for r0 in range(0, nk, rows):
        bucket = bucket_ref[r0:r0 + rows, :]
        accs = [jnp.where(bucket < 0, -jnp.inf, 0.0).astype(F32)] * SWA_HEADS
        for i in range(N_BUCKETS):
            hit = bucket == i
            accs = [jnp.where(hit, rb_ref[i, h] * LOG2E, a) for h, a in enumerate(accs)]
        for h, a in enumerate(accs):
            g, j = divmod(h, SWA_GROUP)
            o_ref[1, g, r0:r0 + rows, j * nq:(j + 1) * nq] = a
            o_ref[0, g, r0:r0 + rows, j * nq:(j + 1) * nq] = jnp.full_like(a, -jnp.inf) if r0 < WINDOW else a


def _bias_rows_kernel(bucket_ref, rb_ref, o_ref):
    nq = bucket_ref.shape[0]
    bucket = bucket_ref[...]
    for h in range(SWA_HEADS):
        g, j = divmod(h, SWA_GROUP)
        o_ref[g, j * nq:(j + 1) * nq, :] = _bias_select(bucket, rb_ref, h)


def _bias_table(body, bucket, rel_bias, out_shape, *, name):
    nd_in, nd_out = bucket.ndim, len(out_shape)
    return pl.pallas_call(
        body,
        in_specs=[pl.BlockSpec(bucket.shape, lambda: (0,) * nd_in), pl.BlockSpec(memory_space=pltpu.SMEM)],
        out_specs=pl.BlockSpec(out_shape, lambda: (0,) * nd_out),
        out_shape=jax.ShapeDtypeStruct(out_shape, F32),
        name=name,
    )(bucket, rel_bias)


def _swa_prompt_kernel(q_ref, kxp_ref, kxc_ref, vtp_ref, vtc_ref, bias_ref, sink_ref, hmask_ref, o_ref, ot_ref):
    kx = jnp.concatenate([kxp_ref[0], kxc_ref[0]], axis=0)
    vt = jnp.concatenate([vtp_ref[...], vtc_ref[...]], axis=1)
    gw = SWA_GROUP * SWA_HD

    def scores(g):
        qg = q_ref[0, :, g * gw:(g + 1) * gw]
        qm = jnp.concatenate([qg * hmask_ref[j] for j in range(SWA_GROUP)], axis=0)
        return _dot_nt(kx[:, g * gw:(g + 1) * gw], qm) + bias_ref[0, g]

    def attend(g, s):
        sink = sink_ref[g] * LOG2E
        m = jnp.maximum(jnp.max(s, axis=0, keepdims=True), sink)
        p = jnp.exp2(s - m)
        denom = jnp.sum(p, axis=0, keepdims=True) + jnp.exp2(sink - m)
        ot = _dot(vt[g * SWA_HD:(g + 1) * SWA_HD, :], p.astype(BF16)) / denom
        for j in range(SWA_GROUP):
            h = g * SWA_GROUP + j
            ot_ref[h * SWA_HD:(h + 1) * SWA_HD, :] = ot[:, j * SWA_QBLOCK:(j + 1) * SWA_QBLOCK]

    s_next = scores(0)
    for g in range(SWA_KV_HEADS):
        s_cur = s_next
        if g + 1 < SWA_KV_HEADS:
            s_next = scores(g + 1)
        attend(g, s_cur)
    o_ref[0] = ot_ref[...].T.astype(BF16)


def _swa_prompt(sq, kx, svt, bias, sink_rows, hmask, *, name):
    nb, s, _ = sq.shape
    nj = s // SWA_QBLOCK
    cur = pl.BlockSpec((1, SWA_QBLOCK, SWA_Q), lambda b, j: (b, j, 0))
    prev = pl.BlockSpec((1, WINDOW, SWA_Q), lambda b, j: (b, jnp.maximum(j - 1, 0), 0))
    vt_cur = pl.BlockSpec((SWA_KV, SWA_QBLOCK), lambda b, j: (0, b * nj + j))
    vt_prev = pl.BlockSpec((SWA_KV, WINDOW), lambda b, j: (0, b * nj + jnp.maximum(j - 1, 0)))
    bias_spec = pl.BlockSpec((1,) + bias.shape[1:], lambda b, j: (jnp.minimum(j, 1), 0, 0, 0))
    return pl.pallas_call(
        _swa_prompt_kernel,
        grid=(nb, nj),
        in_specs=[cur, prev, cur, vt_prev, vt_cur, bias_spec, _resident(sink_rows.shape), _resident(hmask.shape)],
        out_specs=cur,
        out_shape=jax.ShapeDtypeStruct((nb, s, SWA_Q), BF16),
        scratch_shapes=[pltpu.VMEM((SWA_Q, SWA_QBLOCK), F32)],
        compiler_params=_params(("parallel", "parallel")),
        name=name,
    )(sq, kx, kx, svt, svt, bias, sink_rows, hmask)


def _swa_sample_kernel(q_ref, ck_ref, nk_ref, cv_ref, nv_ref, bias_ref, sink_ref, hmask_ref, o_ref):
    nbb, nq = q_ref.shape[0], q_ref.shape[1]
    gw = SWA_GROUP * SWA_HD
    pairs = [(b, g) for b in range(nbb) for g in range(SWA_KV_HEADS)]
    gsl = lambda g: slice(g * SWA_HD, (g + 1) * SWA_HD)
    ks = [jnp.concatenate([ck_ref[b], nk_ref[b]], axis=0).astype(BF16) for b in range(nbb)]
    vs = [jnp.concatenate([cv_ref[b], nv_ref[b]], axis=0).astype(BF16) for b in range(nbb)]
    scores = []
    for b, g in pairs:
        qg = q_ref[b, :, g * gw:(g + 1) * gw]
        qm = jnp.concatenate([qg * hmask_ref[j] for j in range(SWA_GROUP)], axis=0)
        kxg = jnp.concatenate([ks[b][:, gsl(g)]] * SWA_GROUP, axis=1)
        scores.append(_dot_nt(qm, kxg) + bias_ref[g])
    probs = []
    for (b, g), s in zip(pairs, scores):
        sink = sink_ref[g] * LOG2E
        m = jnp.maximum(jnp.max(s, axis=-1, keepdims=True), sink)
        p = jnp.exp2(s - m)
        denom = jnp.sum(p, axis=-1, keepdims=True) + jnp.exp2(sink - m)
        probs.append((p / denom).astype(BF16))
    outs = {}
    for (b, g), pr in zip(pairs, probs):
        og = _dot(pr, vs[b][:, gsl(g)])
        outs[b, g] = [og[j * nq:(j + 1) * nq, :] for j in range(SWA_GROUP)]
    for b in range(nbb):
        o_ref[b] = jnp.concatenate([o for g in range(SWA_KV_HEADS) for o in outs[b, g]], axis=-1).astype(BF16)


def _swa_sample(sq, ck, nk, cv, nv, bias, sink_cols, hmask, *, name):
    nb, l, _ = sq.shape
    n_cache = ck.shape[1]
    nbb = math.gcd(nb, 4)
    blk = lambda r, n: pl.BlockSpec((nbb, r, n), lambda b: (b, 0, 0))
    return pl.pallas_call(
        _swa_sample_kernel,
        grid=(nb // nbb,),
        in_specs=[blk(l, SWA_Q), blk(n_cache, SWA_KV), blk(l, SWA_KV), blk(n_cache, SWA_KV), blk(l, SWA_KV),
                  _resident(bias.shape), _resident(sink_cols.shape), _resident(hmask.shape)],
        out_specs=blk(l, SWA_Q),
        out_shape=jax.ShapeDtypeStruct((nb, l, SWA_Q), BF16),
        compiler_params=_params(("parallel",)),
        name=name,
    )(sq, ck, nk, cv, nv, bias, sink_cols, hmask)


def _merge_kernel(x_ref, og_ref, os_ref, siga_ref, sigb_ref, wba_ref, wbb_ref, wo_ref, o_ref):
    ya = _dot(og_ref[...], wba_ref[...])
    yb = _dot(os_ref[...], wbb_ref[...])
    m = (siga_ref[...].astype(F32) * ya + sigb_ref[...].astype(F32) * yb).astype(BF16)
    o_ref[...] = x_ref[...] + _dot(m, wo_ref[...])


def _merge(x, og, osw, siga, sigb, wba, wbb, wo, *, name):
    t = x.shape[0]
    tm = min(ROW_TILE, t)
    row = pl.BlockSpec((tm, D_MODEL), lambda i: (i, 0))
    return pl.pallas_call(
        _merge_kernel,
        grid=(t // tm,),
        in_specs=[row] * 5 + [_resident(wba.shape), _resident(wbb.shape), _resident(wo.shape)],
        out_specs=row,
        out_shape=jax.ShapeDtypeStruct((t, D_MODEL), F32),
        compiler_params=_params(("parallel",)),
        name=name,
    )(x, og, osw, siga, sigb, wba, wbb, wo)


def _layer_weights(l, ffn1_norm, ffn1_w_up, ffn1_w_down, mix_norm, w_in, gla_w_alpha, gla_b_alpha,
                   gla_head_norm, q_norm, k_norm, w_branch, w_out, ffn2_norm, ffn2_w_up, ffn2_w_down, final_norm):
    splits = (GLA_QK, GLA_QK, GLA_V, GLA_V, GLA_RANK, SWA_Q, SWA_KV, SWA_KV, D_MODEL, D_MODEL)
    offs = np.concatenate([[0], np.cumsum(splits)])
    wi = w_in[l]
    cols = [wi[:, offs[i]:offs[i + 1]].astype(BF16) for i in range(len(splits))]
    hd_group = np.kron(np.eye(SWA_HEADS, dtype=np.float32), np.ones((SWA_HD, SWA_HD), np.float32))
    w = {
        "mix_norm": mix_norm[l][None],
        "wq": cols[0], "wk": cols[1], "wv": cols[2], "wr": cols[3],
        "wa": jnp.pad(cols[4], ((0, 0), (0, RANK_PAD - GLA_RANK))),
        "walpha": jnp.pad(gla_w_alpha[l].astype(BF16), ((0, RANK_PAD - GLA_RANK), (0, 0))),
        "balpha": gla_b_alpha[l][None],
        "wsq": cols[5], "wsk": cols[6], "wsv": cols[7], "wga": cols[8], "wgb": cols[9],
        "gq": jnp.asarray(hd_group, BF16),
        "gk": jnp.asarray(hd_group[:SWA_KV, :SWA_KV], BF16),
        "qn": jnp.tile(q_norm[l], SWA_HEADS)[None],
        "kn": jnp.tile(k_norm[l], SWA_KV_HEADS)[None],
    }
    ffn1 = (ffn1_norm[l][None], ffn1_w_up[l][:, :D_FF].astype(BF16), ffn1_w_up[l][:, D_FF:].astype(BF16),
            ffn1_w_down[l].astype(BF16))
    ffn2 = (ffn2_norm[l][None], ffn2_w_up[l][:, :D_FF].astype(BF16), ffn2_w_up[l][:, D_FF:].astype(BF16),
            ffn2_w_down[l].astype(BF16))
    merge = (w_branch[l][:GLA_V].astype(BF16), w_branch[l][GLA_V:].astype(BF16), w_out[l].astype(BF16))
    return w, ffn1, ffn2, merge, gla_head_norm[l][None], final_norm[l][None]


def kernel(x_prompt, x_sample, cache_swa_k, cache_swa_v, state_gla, ffn1_norm, ffn1_w_up, ffn1_w_down, mix_norm, w_in, gla_w_alpha, gla_b_alpha, gla_head_norm, q_norm, k_norm, attn_sinks, rel_bias, w_branch, w_out, ffn2_norm, ffn2_w_up, ffn2_w_down, final_norm):
    nbp, s, _ = x_prompt.shape
    nbs, l, _ = x_sample.shape
    depth = ffn1_norm.shape[0]
    n_cache = cache_swa_k.shape[2]
    past = PAST_LEN
    assert s % SWA_QBLOCK == 0 and s % GLA_BLOCK == 0 and n_cache == WINDOW and s >= WINDOW

    tq = jnp.arange(SWA_QBLOCK)
    tk = jnp.arange(SWA_KBLOCK) - WINDOW
    qc, kc = tq // CHUNK, tk // CHUNK
    vis_p = (kc[None, :] <= qc[:, None]) & (kc[None, :] >= qc[:, None] - WIN_CHUNKS)
    bucket_p = jnp.where(vis_p, _t5_bucket(tk[None, :] - tq[:, None]), -1).astype(jnp.int32)
    qpos_s = past + jnp.arange(l)
    kpos_s = jnp.concatenate([past - n_cache + jnp.arange(n_cache), qpos_s])
    qcs, kcs = qpos_s // CHUNK, kpos_s // CHUNK
    vis_s = (kpos_s[None, :] >= 0) & (kcs[None, :] <= qcs[:, None]) & (kcs[None, :] >= qcs[:, None] - WIN_CHUNKS)
    bucket_s = jnp.where(vis_s, _t5_bucket(kpos_s[None, :] - qpos_s[:, None]), -1).astype(jnp.int32)
    gq = SWA_GROUP * SWA_QBLOCK
    bias_p = _bias_table(_bias_cols_kernel, bucket_p.T, rel_bias, (2, SWA_KV_HEADS, SWA_KBLOCK, gq),
                         name="bias_prompt")
    bias_s = _bias_table(_bias_rows_kernel, bucket_s, rel_bias, (SWA_KV_HEADS, SWA_GROUP * l, n_cache + l),
                         name="bias_sample")
    head_of_lane = jnp.arange(SWA_GROUP * SWA_HD) // SWA_HD
    hmask = (head_of_lane[None, None, :] == jnp.arange(SWA_GROUP)[:, None, None]).astype(BF16)
    hmask_p = jnp.broadcast_to(hmask, (SWA_GROUP, SWA_QBLOCK, SWA_GROUP * SWA_HD))
    hmask_s = jnp.broadcast_to(hmask, (SWA_GROUP, l, SWA_GROUP * SWA_HD))

    xp = x_prompt.reshape(nbp * s, D_MODEL)
    xs = x_sample.reshape(nbs * l, D_MODEL)
    outs = [[] for _ in range(6)]
    for layer in range(depth):
        w, ffn1, ffn2, mrg, hn, fn = _layer_weights(
            layer, ffn1_norm, ffn1_w_up, ffn1_w_down, mix_norm, w_in, gla_w_alpha, gla_b_alpha,
            gla_head_norm, q_norm, k_norm, w_branch, w_out, ffn2_norm, ffn2_w_up, ffn2_w_down, final_norm)
        sinks = attn_sinks[layer]
        sink_rows = jnp.repeat(sinks, SWA_QBLOCK).reshape(SWA_KV_HEADS, 1, gq)
        sink_cols = jnp.repeat(sinks, l).reshape(SWA_KV_HEADS, SWA_GROUP * l, 1)

        xp = _ffn(xp, *ffn1, fn, final_norm=False, name="ffn1_prompt")
        qt, kt, v, sgr, e, sq, sk, sv, siga, sigb, kx, svt = _proj(xp, w, chunk=GLA_BLOCK, name="proj_prompt")
        e = e.reshape(2, nbp, s // GLA_BLOCK, 1, GLA_QK)
        r3 = lambda a: a.reshape(nbp, s, a.shape[-1])
        og, s_fin = _gla(r3(qt), r3(kt), r3(v), r3(sgr), e,
                         jnp.zeros((nbp, GLA_HEADS, GLA_DK, GLA_DV), F32), hn, blk=GLA_BLOCK, name="gla_prompt")
        osw = _swa_prompt(r3(sq), r3(kx), svt, bias_p, sink_rows, hmask_p, name="swa_prompt")
        xp = _merge(xp, og.reshape(nbp * s, GLA_V), osw.reshape(nbp * s, SWA_Q), siga, sigb, *mrg,
                    name="merge_prompt")
        n_keep = min(WINDOW, s)
        outs[0].append(r3(sk)[:, s - n_keep:].reshape(nbp, n_keep, SWA_KV_HEADS, SWA_HD))
        outs[1].append(r3(sv)[:, s - n_keep:].reshape(nbp, n_keep, SWA_KV_HEADS, SWA_HD))
        outs[2].append(s_fin)
        xp = _ffn(xp, *ffn2, fn, final_norm=True, name="ffn2_prompt")

        xs = _ffn(xs, *ffn1, fn, final_norm=False, name="ffn1_sample")
        qt, kt, v, sgr, e, sq, sk, sv, siga, sigb, _, _ = _proj(xs, w, chunk=l, name="proj_sample")
        e = e.reshape(2, nbs, 1, 1, GLA_QK)
        r3 = lambda a: a.reshape(nbs, l, a.shape[-1])
        og, s_new = _gla(r3(qt), r3(kt), r3(v), r3(sgr), e, state_gla[layer], hn, blk=l, name="gla_sample")
        ck = cache_swa_k[layer].reshape(nbs, n_cache, SWA_KV)
        cv = cache_swa_v[layer].reshape(nbs, n_cache, SWA_KV)
        osw = _swa_sample(r3(sq), ck, r3(sk), cv, r3(sv), bias_s, sink_cols, hmask_s, name="swa_sample")
        xs = _merge(xs, og.reshape(nbs * l, GLA_V), osw.reshape(nbs * l, SWA_Q), siga, sigb, *mrg,
                    name="merge_sample")
        outs[3].append(r3(sk).reshape(nbs, l, SWA_KV_HEADS, SWA_HD))
        outs[4].append(r3(sv).reshape(nbs, l, SWA_KV_HEADS, SWA_HD))
        outs[5].append(s_new)
        xs = _ffn(xs, *ffn2, fn, final_norm=True, name="ffn2_sample")

    return (xp.reshape(nbp, s, D_MODEL), xs.reshape(nbs, l, D_MODEL),
            jnp.stack(outs[0]), jnp.stack(outs[1]), jnp.stack(outs[2]),
            jnp.stack(outs[3]), jnp.stack(outs[4]), jnp.stack(outs[5]))
```

```python
import functools
import math

import numpy as np
import jax
import jax.numpy as jnp
from jax import lax
from jax.experimental import pallas as pl
from jax.experimental.pallas import tpu as pltpu

F32 = jnp.float32
BF16 = jnp.bfloat16

D_MODEL = 1024
PAST_LEN = 2048
CHUNK = 64
GLA_HEADS = 4
GLA_DK = 128
GLA_DV = 256
GLA_RANK = 16
GLA_TAU = 16.0
SWA_HEADS = 16
SWA_KV_HEADS = 4
SWA_HD = 64
SWA_GROUP = SWA_HEADS // SWA_KV_HEADS
WINDOW = 128
WIN_CHUNKS = WINDOW // CHUNK
N_BUCKETS = 32
MAX_DISTANCE = 128
D_FF = 2816
EPS = 1e-6
LOG2E = 1.4426950408889634
GLA_QK = GLA_HEADS * GLA_DK
GLA_V = GLA_HEADS * GLA_DV
SWA_Q = SWA_HEADS * SWA_HD
SWA_KV = SWA_KV_HEADS * SWA_HD

LANES = 128
RANK_PAD = LANES
VMEM_LIMIT = 56 * 1024 * 1024
ROW_TILE = 512
GLA_BLOCK = 128
GLA_LEAF = 8
SWA_QBLOCK = 2 * CHUNK
SWA_KBLOCK = SWA_QBLOCK + WINDOW


def _dot(a, b):
    return jnp.dot(a, b, preferred_element_type=F32)


def _dot_nt(a, b):
    return lax.dot_general(a, b, (((1,), (1,)), ((), ())), preferred_element_type=F32)


def _dot_tn(a, b):
    return lax.dot_general(a, b, (((0,), (0,)), ((), ())), preferred_element_type=F32)


def _rms(x):
    return x * lax.rsqrt(jnp.mean(x * x, axis=-1, keepdims=True) + EPS)


def _resident(shape):
    nd = len(shape)
    return pl.BlockSpec(shape, lambda *_: (0,) * nd, pipeline_mode=pl.Buffered(1))


def _params(sem):
    return pltpu.CompilerParams(dimension_semantics=sem, vmem_limit_bytes=VMEM_LIMIT)


def _ffn_kernel(x_ref, g_ref, wa_ref, wb_ref, wd_ref, fg_ref, o_ref, *, final_norm):
    x = x_ref[...]
    h = (_rms(x) * g_ref[...]).astype(BF16)
    a = _dot(h, wa_ref[...])
    b = _dot(h, wb_ref[...])
    act = (a * jax.nn.sigmoid(a) * b).astype(BF16)
    y = x + 0.5 * _dot(act, wd_ref[...])
    if final_norm:
        y = _rms(y) * fg_ref[...]
    o_ref[...] = y


def _ffn(x, g, wa, wb, wd, fg, *, final_norm, name):
    t = x.shape[0]
    tm = min(ROW_TILE, t)
    row = pl.BlockSpec((tm, D_MODEL), lambda i: (i, 0))
    return pl.pallas_call(
        functools.partial(_ffn_kernel, final_norm=final_norm),
        grid=(t // tm,),
        in_specs=[row, _resident(g.shape), _resident(wa.shape), _resident(wb.shape),
                  _resident(wd.shape), _resident(fg.shape)],
        out_specs=row,
        out_shape=jax.ShapeDtypeStruct((t, D_MODEL), F32),
        compiler_params=_params(("parallel",)),
        name=name,
    )(x, g, wa, wb, wd, fg)


def _cumsum_matrix(tm, chunk):
    nc = tm // chunk
    m = np.zeros((tm + nc, tm), np.float32)
    for c in range(nc):
        o = c * chunk
        m[o:o + chunk, o:o + chunk] = np.tril(np.ones((chunk, chunk), np.float32))
        m[tm + c, o:o + chunk] = 1.0
    return m


def _gla_levels(chunk):
    return tuple(n for n in (128, 64, 32, 16) if GLA_LEAF < n <= chunk)


def _proj_kernel(*refs, tm, chunk, levels):
    nl = len(levels)
    (x_ref, g_ref, wq_ref, wk_ref, wv_ref, wr_ref, wa_ref, wal_ref, bal_ref,
     wsq_ref, wsk_ref, wsv_ref, wga_ref, wgb_ref, gq_ref, gk_ref, qn_ref, kn_ref, cm_ref) = refs[:19]
    qi_ref, kd_ref = refs[19:21]
    lvl_refs = refs[21:21 + nl]
    (ad_ref, v_ref, sgr_ref, el_ref, sq_ref, sk_ref, sv_ref, siga_ref, sigb_ref,
     kx_ref, svt_ref) = refs[21 + nl:]
    nc = tm // chunk
    h = (_rms(x_ref[...]) * g_ref[...]).astype(BF16)

    ga = _dot(h, wa_ref[...])
    z = _dot(ga.astype(BF16), wal_ref[...]) + bal_ref[...]
    la = (jnp.minimum(z, 0.0) - jnp.log1p(jnp.exp(-jnp.abs(z)))) * (1.0 / GLA_TAU)
    la_hi = la.astype(BF16)
    la_lo = (la - la_hi.astype(F32)).astype(BF16)
    bb = _dot(cm_ref[...], la_hi) + _dot(cm_ref[...], la_lo)
    b = bb[:tm]
    b_last = bb[tm:]
    el_ref[0] = jnp.exp(b_last)
    qs = _dot(h, wq_ref[...]) * (GLA_DK ** -0.5)
    k = _dot(h, wk_ref[...])
    qi_ref[...] = (qs * jnp.exp(b)).astype(BF16)
    b3 = b.reshape(nc, chunk, GLA_QK)
    kd_ref[...] = (k.reshape(nc, chunk, GLA_QK) * jnp.exp(b_last[:, None, :] - b3)).reshape(tm, GLA_QK).astype(BF16)

    for n, ref in zip(levels, lvl_refs):
        hf = n // 2
        bn = b.reshape(tm // n, n, GLA_QK)
        r = bn[:, hf - 1:hf, :]
        k_side = k.reshape(tm // n, n, GLA_QK)[:, :hf, :] * jnp.exp(r - bn[:, :hf, :])
        q_side = qs.reshape(tm // n, n, GLA_QK)[:, hf:, :] * jnp.exp(bn[:, hf:, :] - r)
        ref[...] = jnp.concatenate([k_side, q_side], axis=1).reshape(tm, GLA_QK).astype(BF16)

    nblk = tm // GLA_LEAF
    a3 = jnp.exp(la).reshape(nblk, GLA_LEAF, GLA_QK)
    k3 = k.reshape(nblk, GLA_LEAF, GLA_QK)
    q3 = qs.reshape(nblk, GLA_LEAF, GLA_QK)
    sub = lax.broadcasted_iota(jnp.int32, (nblk, GLA_LEAF, GLA_QK), 1)
    lane = lax.broadcasted_iota(jnp.int32, (tm, LANES), 1)
    ad = [jnp.zeros((tm, LANES), F32) for _ in range(GLA_HEADS)]
    e = None
    for off in range(GLA_LEAF):
        if off == 0:
            p = q3 * k3
        else:
            step = a3 if off == 1 else pltpu.roll(a3, off - 1, 1)
            e = step if off == 1 else e * step
            p = jnp.where(sub >= off, q3 * pltpu.roll(k3, off, 1) * e, 0.0)
        p = p.reshape(tm, GLA_QK)
        for hh in range(GLA_HEADS):
            col = jnp.sum(p[:, hh * GLA_DK:(hh + 1) * GLA_DK], axis=1, keepdims=True)
            ad[hh] = jnp.where(lane == (LANES - off) % LANES, col, ad[hh])
    ad_ref[...] = jnp.concatenate(ad, axis=1)

    v_ref[...] = _dot(h, wv_ref[...]).astype(BF16)
    rr = _dot(h, wr_ref[...])
    sgr_ref[...] = (rr * jax.nn.sigmoid(rr)).astype(BF16)

    uq = _dot(h, wsq_ref[...])
    ssq = _dot((uq * uq).astype(BF16), gq_ref[...])
    sqn = uq * lax.rsqrt(ssq * (1.0 / SWA_HD) + EPS) * qn_ref[...]
    sq_ref[...] = (sqn * (SWA_HD ** -0.5 * LOG2E)).astype(BF16)
    uk = _dot(h, wsk_ref[...])
    ssk = _dot((uk * uk).astype(BF16), gk_ref[...])
    skn = uk * lax.rsqrt(ssk * (1.0 / SWA_HD) + EPS) * kn_ref[...]
    sk_ref[...] = skn
    uv = _dot(h, wsv_ref[...])
    sv_ref[...] = uv
    kx_ref[...] = jnp.concatenate(
        [skn[:, g * SWA_HD:(g + 1) * SWA_HD] for g in range(SWA_KV_HEADS) for _ in range(SWA_GROUP)],
        axis=1).astype(BF16)
    svt_ref[...] = uv.T.astype(BF16)

    siga_ref[...] = jax.nn.sigmoid(_dot(h, wga_ref[...])).astype(BF16)
    sigb_ref[...] = jax.nn.sigmoid(_dot(h, wgb_ref[...])).astype(BF16)


def _proj(x, w, *, chunk, name):
    t = x.shape[0]
    tm = min(ROW_TILE, t)
    nt = t // tm
    nc = tm // chunk
    levels = _gla_levels(chunk)
    cm = jnp.asarray(_cumsum_matrix(tm, chunk), BF16)
    ins = [w["mix_norm"], w["wq"], w["wk"], w["wv"], w["wr"], w["wa"], w["walpha"], w["balpha"],
           w["wsq"], w["wsk"], w["wsv"], w["wga"], w["wgb"], w["gq"], w["gk"], w["qn"], w["kn"], cm]

    def row(n):
        return pl.BlockSpec((tm, n), lambda i: (i, 0))

    names = (["qi", "kd"] + ["lvl%d" % n for n in levels] +
             ["ad", "v", "sgr", "el", "sq", "sk", "sv", "siga", "sigb", "kx", "svt"])
    widths = {"qi": (GLA_QK, BF16), "kd": (GLA_QK, BF16), "ad": (GLA_HEADS * LANES, F32), "v": (GLA_V, BF16),
              "sgr": (GLA_V, BF16), "sq": (SWA_Q, BF16), "sk": (SWA_KV, F32), "sv": (SWA_KV, F32),
              "siga": (D_MODEL, BF16), "sigb": (D_MODEL, BF16), "kx": (SWA_Q, BF16)}
    out_shape, out_specs = [], []
    for nm in names:
        if nm == "el":
            out_shape.append(jax.ShapeDtypeStruct((nt, nc, GLA_QK), F32))
            out_specs.append(pl.BlockSpec((1, nc, GLA_QK), lambda i: (i, 0, 0)))
        elif nm == "svt":
            out_shape.append(jax.ShapeDtypeStruct((SWA_KV, t), BF16))
            out_specs.append(pl.BlockSpec((SWA_KV, tm), lambda i: (0, i)))
        else:
            n, dt = widths.get(nm, (GLA_QK, BF16))
            out_shape.append(jax.ShapeDtypeStruct((t, n), dt))
            out_specs.append(row(n))
    res = pl.pallas_call(
        functools.partial(_proj_kernel, tm=tm, chunk=chunk, levels=levels),
        grid=(nt,),
        in_specs=[row(D_MODEL)] + [_resident(a.shape) for a in ins],
        out_specs=out_specs,
        out_shape=out_shape,
        compiler_params=_params(("parallel",)),
        name=name,
    )(x, *ins)
    return dict(zip(names, res)), levels


def _gla_kernel(*refs, nb, blk, levels):
    nl = len(levels)
    qi_ref, kd_ref = refs[:2]
    lvl_refs = refs[2:2 + nl]
    ad_ref, v_ref, sgr_ref, el_ref, s0_ref, hn_ref, o_ref, sfin_ref, st_ref = refs[2 + nl:]
    c = pl.program_id(0)

    @pl.when(c == 0)
    def _():
        for b in range(nb):
            for h in range(GLA_HEADS):
                st_ref[b, h] = s0_ref[b, h].T

    rows = lax.broadcasted_iota(jnp.int32, (blk, blk), 0)
    cols = lax.broadcasted_iota(jnp.int32, (blk, blk), 1)
    masks = [(rows // n == cols // n) & (rows % n >= n // 2) & (cols % n < n // 2) for n in levels]
    hn = hn_ref[...]
    for b in range(nb):
        for h in range(GLA_HEADS):
            ks = slice(h * GLA_DK, (h + 1) * GLA_DK)
            vs = slice(h * GLA_DV, (h + 1) * GLA_DV)
            vb = v_ref[b, :, vs]
            a = pltpu.roll(ad_ref[b, :, h * LANES:(h + 1) * LANES], 0, 1, stride=1, stride_axis=0)[:, :blk]
            for mask, ref in zip(masks, lvl_refs):
                mn = ref[b, :, ks]
                a = a + jnp.where(mask, _dot_nt(mn, mn), 0.0)
            s_t = st_ref[b, h]
            o = _dot_nt(qi_ref[b, :, ks], s_t.astype(BF16)) + _dot(a.astype(BF16), vb)
            st_ref[b, h] = s_t * el_ref[b, 0, :, ks] + _dot_tn(vb, kd_ref[b, :, ks])
            y = _rms(o) * hn * sgr_ref[b, :, vs].astype(F32)
            o_ref[b, :, vs] = y.astype(BF16)

    @pl.when(c == pl.num_programs(0) - 1)
    def _():
        for b in range(nb):
            for h in range(GLA_HEADS):
                sfin_ref[b, h] = st_ref[b, h].T


def _gla(p, levels, s0, hn, *, nb, blk, name):
    s = p["qi"].shape[0] // nb
    nc = s // blk
    r3 = lambda a: a.reshape(nb, s, a.shape[-1])
    seqs = [r3(p["qi"]), r3(p["kd"])] + [r3(p["lvl%d" % n]) for n in levels] + [r3(p["ad"]), r3(p["v"]), r3(p["sgr"])]
    el = p["el"].reshape(nb, nc, 1, GLA_QK)

    def seq(a):
        return pl.BlockSpec((nb, blk, a.shape[-1]), lambda c: (0, c, 0))

    return pl.pallas_call(
        functools.partial(_gla_kernel, nb=nb, blk=blk, levels=levels),
        grid=(nc,),
        in_specs=[seq(a) for a in seqs] + [pl.BlockSpec((nb, 1, 1, GLA_QK), lambda c: (0, c, 0, 0)),
                                           _resident(s0.shape), _resident(hn.shape)],
        out_specs=[pl.BlockSpec((nb, blk, GLA_V), lambda c: (0, c, 0)),
                   pl.BlockSpec(s0.shape, lambda c: (0, 0, 0, 0))],
        out_shape=[jax.ShapeDtypeStruct((nb, s, GLA_V), BF16), jax.ShapeDtypeStruct(s0.shape, F32)],
        scratch_shapes=[pltpu.VMEM((nb, GLA_HEADS, GLA_DV, GLA_DK), F32)],
        compiler_params=_params(("arbitrary",)),
        name=name,
    )(*seqs, el, s0, hn)


def _t5_bucket(rel):
    nb = N_BUCKETS // 2
    ret = jnp.where(rel > 0, nb, 0)
    n = jnp.abs(rel)
    max_exact = nb // 2
    nf = jnp.maximum(n, 1).astype(jnp.float32)
    large = max_exact + (jnp.log(nf / max_exact) / math.log(MAX_DISTANCE / max_exact)
                         * (nb - max_exact)).astype(jnp.int32)
    large = jnp.minimum(large, nb - 1)
    return ret + jnp.where(n < max_exact, n, large)


def _bias_select(bucket, rb_ref, h):
    acc = jnp.where(bucket < 0, -jnp.inf, 0.0).astype(F32)
    for i in range(N_BUCKETS):
        acc = jnp.where(bucket == i, rb_ref[i, h] * LOG2E, acc)
    return acc


def _bias_cols_kernel(bucket_ref, rb_ref, o_ref):
    nk, nq = bucket_ref.shape
    rows = 16
    for r0 in range(0, nk, rows):
        bucket = bucket_ref[r0:r0 + rows, :]
        accs = [jnp.where(bucket < 0, -jnp.inf, 0.0).astype(F32)] * SWA_HEADS
        for i in range(N_BUCKETS):
            hit = bucket == i
            accs = [jnp.where(hit, rb_ref[i, h] * LOG2E, a) for h, a in enumerate(accs)]
        for h, a in enumerate(accs):
            g, j = divmod(h, SWA_GROUP)
            o_ref[1, g, r0:r0 + rows, j * nq:(j + 1) * nq] = a
            o_ref[0, g, r0:r0 + rows, j * nq:(j + 1) * nq] = jnp.full_like(a, -jnp.inf) if r0 < WINDOW else a


def _bias_rows_kernel(bucket_ref, rb_ref, o_ref):
    nq = bucket_ref.shape[0]
    bucket = bucket_ref[...]
    for h in range(SWA_HEADS):
        g, j = divmod(h, SWA_GROUP)
        o_ref[g, j * nq:(j + 1) * nq, :] = _bias_select(bucket, rb_ref, h)


def _bias_table(body, bucket, rel_bias, out_shape, *, name):
    nd_in, nd_out = bucket.ndim, len(out_shape)
    return pl.pallas_call(
        body,
        in_specs=[pl.BlockSpec(bucket.shape, lambda: (0,) * nd_in), pl.BlockSpec(memory_space=pltpu.SMEM)],
        out_specs=pl.BlockSpec(out_shape, lambda: (0,) * nd_out),
        out_shape=jax.ShapeDtypeStruct(out_shape, F32),
        name=name,
    )(bucket, rel_bias)


def _swa_prompt_kernel(q_ref, kxp_ref, kxc_ref, vtp_ref, vtc_ref, bias_ref, sink_ref, hmask_ref, o_ref, ot_ref):
    kx = jnp.concatenate([kxp_ref[0], kxc_ref[0]], axis=0)
    vt = jnp.concatenate([vtp_ref[...], vtc_ref[...]], axis=1)
    gw = SWA_GROUP * SWA_HD

    def scores(g):
        qg = q_ref[0, :, g * gw:(g + 1) * gw]
        qm = jnp.concatenate([qg * hmask_ref[j] for j in range(SWA_GROUP)], axis=0)
        return _dot_nt(kx[:, g * gw:(g + 1) * gw], qm) + bias_ref[0, g]

    def attend(g, s):
        sink = sink_ref[g] * LOG2E
        m = jnp.maximum(jnp.max(s, axis=0, keepdims=True), sink)
        p = jnp.exp2(s - m)
        denom = jnp.sum(p, axis=0, keepdims=True) + jnp.exp2(sink - m)
        ot = _dot(vt[g * SWA_HD:(g + 1) * SWA_HD, :], p.astype(BF16)) / denom
        for j in range(SWA_GROUP):
            h = g * SWA_GROUP + j
            ot_ref[h * SWA_HD:(h + 1) * SWA_HD, :] = ot[:, j * SWA_QBLOCK:(j + 1) * SWA_QBLOCK]

    s_next = scores(0)
    for g in range(SWA_KV_HEADS):
        s_cur = s_next
        if g + 1 < SWA_KV_HEADS:
            s_next = scores(g + 1)
        attend(g, s_cur)
    o_ref[0] = ot_ref[...].T.astype(BF16)


def _swa_prompt(sq, kx, svt, bias, sink_rows, hmask, *, name):
    nb, s, _ = sq.shape
    nj = s // SWA_QBLOCK
    cur = pl.BlockSpec((1, SWA_QBLOCK, SWA_Q), lambda b, j: (b, j, 0))
    prev = pl.BlockSpec((1, WINDOW, SWA_Q), lambda b, j: (b, jnp.maximum(j - 1, 0), 0))
    vt_cur = pl.BlockSpec((SWA_KV, SWA_QBLOCK), lambda b, j: (0, b * nj + j))
    vt_prev = pl.BlockSpec((SWA_KV, WINDOW), lambda b, j: (0, b * nj + jnp.maximum(j - 1, 0)))
    bias_spec = pl.BlockSpec((1,) + bias.shape[1:], lambda b, j: (jnp.minimum(j, 1), 0, 0, 0))
    return pl.pallas_call(
        _swa_prompt_kernel,
        grid=(nb, nj),
        in_specs=[cur, prev, cur, vt_prev, vt_cur, bias_spec, _resident(sink_rows.shape), _resident(hmask.shape)],
        out_specs=cur,
        out_shape=jax.ShapeDtypeStruct((nb, s, SWA_Q), BF16),
        scratch_shapes=[pltpu.VMEM((SWA_Q, SWA_QBLOCK), F32)],
        compiler_params=_params(("parallel", "parallel")),
        name=name,
    )(sq, kx, kx, svt, svt, bias, sink_rows, hmask)


def _swa_sample_kernel(q_ref, ck_ref, nk_ref, cv_ref, nv_ref, bias_ref, sink_ref, hmask_ref, o_ref):
    nbb, nq = q_ref.shape[0], q_ref.shape[1]
    gw = SWA_GROUP * SWA_HD
    pairs = [(b, g) for b in range(nbb) for g in range(SWA_KV_HEADS)]
    gsl = lambda g: slice(g * SWA_HD, (g + 1) * SWA_HD)
    ks = [jnp.concatenate([ck_ref[b], nk_ref[b]], axis=0).astype(BF16) for b in range(nbb)]
    vs = [jnp.concatenate([cv_ref[b], nv_ref[b]], axis=0).astype(BF16) for b in range(nbb)]
    scores = []
    for b, g in pairs:
        qg = q_ref[b, :, g * gw:(g + 1) * gw]
        qm = jnp.concatenate([qg * hmask_ref[j] for j in range(SWA_GROUP)], axis=0)
        kxg = jnp.concatenate([ks[b][:, gsl(g)]] * SWA_GROUP, axis=1)
        scores.append(_dot_nt(qm, kxg) + bias_ref[g])
    probs = []
    for (b, g), s in zip(pairs, scores):
        sink = sink_ref[g] * LOG2E
        m = jnp.maximum(jnp.max(s, axis=-1, keepdims=True), sink)
        p = jnp.exp2(s - m)
        denom = jnp.sum(p, axis=-1, keepdims=True) + jnp.exp2(sink - m)
        probs.append((p / denom).astype(BF16))
    outs = {}
    for (b, g), pr in zip(pairs, probs):
        og = _dot(pr, vs[b][:, gsl(g)])
        outs[b, g] = [og[j * nq:(j + 1) * nq, :] for j in range(SWA_GROUP)]
    for b in range(nbb):
        o_ref[b] = jnp.concatenate([o for g in range(SWA_KV_HEADS) for o in outs[b, g]], axis=-1).astype(BF16)


def _swa_sample(sq, ck, nk, cv, nv, bias, sink_cols, hmask, *, name):
    nb, l, _ = sq.shape
    n_cache = ck.shape[1]
    nbb = math.gcd(nb, 4)
    blk = lambda r, n: pl.BlockSpec((nbb, r, n), lambda b: (b, 0, 0))
    return pl.pallas_call(
        _swa_sample_kernel,
        grid=(nb // nbb,),
        in_specs=[blk(l, SWA_Q), blk(n_cache, SWA_KV), blk(l, SWA_KV), blk(n_cache, SWA_KV), blk(l, SWA_KV),
                  _resident(bias.shape), _resident(sink_cols.shape), _resident(hmask.shape)],
        out_specs=blk(l, SWA_Q),
        out_shape=jax.ShapeDtypeStruct((nb, l, SWA_Q), BF16),
        compiler_params=_params(("parallel",)),
        name=name,
    )(sq, ck, nk, cv, nv, bias, sink_cols, hmask)


def _merge_kernel(x_ref, og_ref, os_ref, siga_ref, sigb_ref, wba_ref, wbb_ref, wo_ref, o_ref):
    ya = _dot(og_ref[...], wba_ref[...])
    yb = _dot(os_ref[...], wbb_ref[...])
    m = (siga_ref[...].astype(F32) * ya + sigb_ref[...].astype(F32) * yb).astype(BF16)
    o_ref[...] = x_ref[...] + _dot(m, wo_ref[...])


def _merge(x, og, osw, siga, sigb, wba, wbb, wo, *, name):
    t = x.shape[0]
    tm = min(ROW_TILE, t)
    row = pl.BlockSpec((tm, D_MODEL), lambda i: (i, 0))
    return pl.pallas_call(
        _merge_kernel,
        grid=(t // tm,),
        in_specs=[row] * 5 + [_resident(wba.shape), _resident(wbb.shape), _resident(wo.shape)],
        out_specs=row,
        out_shape=jax.ShapeDtypeStruct((t, D_MODEL), F32),
        compiler_params=_params(("parallel",)),
        name=name,
    )(x, og, osw, siga, sigb, wba, wbb, wo)


def _layer_weights(l, ffn1_norm, ffn1_w_up, ffn1_w_down, mix_norm, w_in, gla_w_alpha, gla_b_alpha,
                   gla_head_norm, q_norm, k_norm, w_branch, w_out, ffn2_norm, ffn2_w_up, ffn2_w_down, final_norm):
    splits = (GLA_QK, GLA_QK, GLA_V, GLA_V, GLA_RANK, SWA_Q, SWA_KV, SWA_KV, D_MODEL, D_MODEL)
    offs = np.concatenate([[0], np.cumsum(splits)])
    wi = w_in[l]
    cols = [wi[:, offs[i]:offs[i + 1]].astype(BF16) for i in range(len(splits))]
    hd_group = np.kron(np.eye(SWA_HEADS, dtype=np.float32), np.ones((SWA_HD, SWA_HD), np.float32))
    w = {
        "mix_norm": mix_norm[l][None],
        "wq": cols[0], "wk": cols[1], "wv": cols[2], "wr": cols[3],
        "wa": jnp.pad(cols[4], ((0, 0), (0, RANK_PAD - GLA_RANK))),
        "walpha": jnp.pad(gla_w_alpha[l].astype(BF16), ((0, RANK_PAD - GLA_RANK), (0, 0))),
        "balpha": gla_b_alpha[l][None],
        "wsq": cols[5], "wsk": cols[6], "wsv": cols[7], "wga": cols[8], "wgb": cols[9],
        "gq": jnp.asarray(hd_group, BF16),
        "gk": jnp.asarray(hd_group[:SWA_KV, :SWA_KV], BF16),
        "qn": jnp.tile(q_norm[l], SWA_HEADS)[None],
        "kn": jnp.tile(k_norm[l], SWA_KV_HEADS)[None],
    }
    ffn1 = (ffn1_norm[l][None], ffn1_w_up[l][:, :D_FF].astype(BF16), ffn1_w_up[l][:, D_FF:].astype(BF16),
            ffn1_w_down[l].astype(BF16))
    ffn2 = (ffn2_norm[l][None], ffn2_w_up[l][:, :D_FF].astype(BF16), ffn2_w_up[l][:, D_FF:].astype(BF16),
            ffn2_w_down[l].astype(BF16))
    merge = (w_branch[l][:GLA_V].astype(BF16), w_branch[l][GLA_V:].astype(BF16), w_out[l].astype(BF16))
    return w, ffn1, ffn2, merge, gla_head_norm[l][None], final_norm[l][None]


def kernel(x_prompt, x_sample, cache_swa_k, cache_swa_v, state_gla, ffn1_norm, ffn1_w_up, ffn1_w_down, mix_norm, w_in, gla_w_alpha, gla_b_alpha, gla_head_norm, q_norm, k_norm, attn_sinks, rel_bias, w_branch, w_out, ffn2_norm, ffn2_w_up, ffn2_w_down, final_norm):
    nbp, s, _ = x_prompt.shape
    nbs, l, _ = x_sample.shape
    depth = ffn1_norm.shape[0]
    n_cache = cache_swa_k.shape[2]
    past = PAST_LEN
    assert s % SWA_QBLOCK == 0 and s % GLA_BLOCK == 0 and n_cache == WINDOW and s >= WINDOW
    assert l % GLA_LEAF == 0 and l <= LANES

    tq = jnp.arange(SWA_QBLOCK)
    tk = jnp.arange(SWA_KBLOCK) - WINDOW
    qc, kc = tq // CHUNK, tk // CHUNK
    vis_p = (kc[None, :] <= qc[:, None]) & (kc[None, :] >= qc[:, None] - WIN_CHUNKS)
    bucket_p = jnp.where(vis_p, _t5_bucket(tk[None, :] - tq[:, None]), -1).astype(jnp.int32)
    qpos_s = past + jnp.arange(l)
    kpos_s = jnp.concatenate([past - n_cache + jnp.arange(n_cache), qpos_s])
    qcs, kcs = qpos_s // CHUNK, kpos_s // CHUNK
    vis_s = (kpos_s[None, :] >= 0) & (kcs[None, :] <= qcs[:, None]) & (kcs[None, :] >= qcs[:, None] - WIN_CHUNKS)
    bucket_s = jnp.where(vis_s, _t5_bucket(kpos_s[None, :] - qpos_s[:, None]), -1).astype(jnp.int32)
    gq = SWA_GROUP * SWA_QBLOCK
    bias_p = _bias_table(_bias_cols_kernel, bucket_p.T, rel_bias, (2, SWA_KV_HEADS, SWA_KBLOCK, gq),
                         name="bias_prompt")
    bias_s = _bias_table(_bias_rows_kernel, bucket_s, rel_bias, (SWA_KV_HEADS, SWA_GROUP * l, n_cache + l),
                         name="bias_sample")
    head_of_lane = jnp.arange(SWA_GROUP * SWA_HD) // SWA_HD
    hmask = (head_of_lane[None, None, :] == jnp.arange(SWA_GROUP)[:, None, None]).astype(BF16)
    hmask_p = jnp.broadcast_to(hmask, (SWA_GROUP, SWA_QBLOCK, SWA_GROUP * SWA_HD))
    hmask_s = jnp.broadcast_to(hmask, (SWA_GROUP, l, SWA_GROUP * SWA_HD))

    xp = x_prompt.reshape(nbp * s, D_MODEL)
    xs = x_sample.reshape(nbs * l, D_MODEL)
    outs = [[] for _ in range(6)]
    for layer in range(depth):
        w, ffn1, ffn2, mrg, hn, fn = _layer_weights(
            layer, ffn1_norm, ffn1_w_up, ffn1_w_down, mix_norm, w_in, gla_w_alpha, gla_b_alpha,
            gla_head_norm, q_norm, k_norm, w_branch, w_out, ffn2_norm, ffn2_w_up, ffn2_w_down, final_norm)
        sinks = attn_sinks[layer]
        sink_rows = jnp.repeat(sinks, SWA_QBLOCK).reshape(SWA_KV_HEADS, 1, gq)
        sink_cols = jnp.repeat(sinks, l).reshape(SWA_KV_HEADS, SWA_GROUP * l, 1)

        xp = _ffn(xp, *ffn1, fn, final_norm=False, name="ffn1_prompt")
        p, levels = _proj(xp, w, chunk=GLA_BLOCK, name="proj_prompt")
        r3 = lambda a: a.reshape(nbp, s, a.shape[-1])
        og, s_fin = _gla(p, levels, jnp.zeros((nbp, GLA_HEADS, GLA_DK, GLA_DV), F32), hn,
                         nb=nbp, blk=GLA_BLOCK, name="gla_prompt")
        osw = _swa_prompt(r3(p["sq"]), r3(p["kx"]), p["svt"], bias_p, sink_rows, hmask_p, name="swa_prompt")
        xp = _merge(xp, og.reshape(nbp * s, GLA_V), osw.reshape(nbp * s, SWA_Q), p["siga"], p["sigb"], *mrg,
                    name="merge_prompt")
        sk, sv = p["sk"], p["sv"]
        n_keep = min(WINDOW, s)
        outs[0].append(r3(sk)[:, s - n_keep:].reshape(nbp, n_keep, SWA_KV_HEADS, SWA_HD))
        outs[1].append(r3(sv)[:, s - n_keep:].reshape(nbp, n_keep, SWA_KV_HEADS, SWA_HD))
        outs[2].append(s_fin)
        xp = _ffn(xp, *ffn2, fn, final_norm=True, name="ffn2_prompt")

        xs = _ffn(xs, *ffn1, fn, final_norm=False, name="ffn1_sample")
        p, levels = _proj(xs, w, chunk=l, name="proj_sample")
        r3 = lambda a: a.reshape(nbs, l, a.shape[-1])
        og, s_new = _gla(p, levels, state_gla[layer], hn, nb=nbs, blk=l, name="gla_sample")
        ck = cache_swa_k[layer].reshape(nbs, n_cache, SWA_KV)
        cv = cache_swa_v[layer].reshape(nbs, n_cache, SWA_KV)
        sk, sv = p["sk"], p["sv"]
        osw = _swa_sample(r3(p["sq"]), ck, r3(sk), cv, r3(sv), bias_s, sink_cols, hmask_s, name="swa_sample")
        xs = _merge(xs, og.reshape(nbs * l, GLA_V), osw.reshape(nbs * l, SWA_Q), p["siga"], p["sigb"], *mrg,
                    name="merge_sample")
        outs[3].append(r3(sk).reshape(nbs, l, SWA_KV_HEADS, SWA_HD))
        outs[4].append(r3(sv).reshape(nbs, l, SWA_KV_HEADS, SWA_HD))
        outs[5].append(s_new)
        xs = _ffn(xs, *ffn2, fn, final_norm=True, name="ffn2_sample")

    return (xp.reshape(nbp, s, D_MODEL), xs.reshape(nbs, l, D_MODEL),
            jnp.stack(outs[0]), jnp.stack(outs[1]), jnp.stack(outs[2]),
            jnp.stack(outs[3]), jnp.stack(outs[4]), jnp.stack(outs[5]))
```
